```python
import jax, jax.numpy as jnp
from jax import lax
import numpy as np

D_MODEL = 2048
BATCH = 8
SEQ = 2048
DEPTH = 2

POOL_WIDTH = D_MODEL // 2
POOL_WINDOWS = (2, 4, 8, 16)
POOL_GROUPS = 4
POOL_GROUP_DIM = POOL_WIDTH // POOL_GROUPS
GMLP_WIDTH = D_MODEL // 2
GMLP_HEADS = 8
GMLP_HEAD_DIM = GMLP_WIDTH // GMLP_HEADS
GMLP_CHUNK = 128
SSM_INNER = D_MODEL
SSM_HEAD_DIM = 64
SSM_HEADS = SSM_INNER // SSM_HEAD_DIM
SSM_GROUPS = 8
SSM_STATE = 128
SSM_CONV = 4
SSM_CHUNK = 128
SSM_CONV_DIM = SSM_INNER + 2 * SSM_GROUPS * SSM_STATE
N_BRANCHES = 3
D_FF = 4 * D_MODEL
N_ADA = 6
IN_SIZES = (POOL_WIDTH, 2 * GMLP_WIDTH, SSM_INNER, SSM_CONV_DIM, SSM_HEADS, N_BRANCHES * D_MODEL)
IN_PROJ_DIM = POOL_WIDTH + 2 * GMLP_WIDTH + SSM_INNER + SSM_CONV_DIM + SSM_HEADS + N_BRANCHES * D_MODEL
IN_SPLITS = (POOL_WIDTH,
             POOL_WIDTH + 2 * GMLP_WIDTH,
             POOL_WIDTH + 2 * GMLP_WIDTH + SSM_INNER,
             POOL_WIDTH + 2 * GMLP_WIDTH + SSM_INNER + SSM_CONV_DIM,
             POOL_WIDTH + 2 * GMLP_WIDTH + SSM_INNER + SSM_CONV_DIM + SSM_HEADS)
RMS_EPS = 1e-6
LN_EPS = 1e-5

kernel_name = 'hybrid_pool_gmlp_ssd_adaln_block'


def rms_norm(x, eps=RMS_EPS):
    xf = x.astype(jnp.float32)
    return (xf * lax.rsqrt(jnp.mean(xf * xf, axis=-1, keepdims=True) + eps)).astype(x.dtype)


def layer_norm(x, g, b, eps=LN_EPS):
    xf = x.astype(jnp.float32)
    mu = jnp.mean(xf, axis=-1, keepdims=True)
    var = jnp.mean(jnp.square(xf - mu), axis=-1, keepdims=True)
    return ((xf - mu) * lax.rsqrt(var + eps)).astype(x.dtype) * g + b


def pool_mixer(p, w_group, scale):
    bsz, s, _ = p.shape
    pf = p.astype(jnp.float32).reshape(bsz, s, POOL_GROUPS, POOL_GROUP_DIM)
    cs = jnp.concatenate([jnp.zeros((bsz, 1, POOL_GROUPS, POOL_GROUP_DIM), jnp.float32),
                          jnp.cumsum(pf, axis=1)], axis=1)
    t = jnp.arange(s)
    means = []
    for g, w in enumerate(POOL_WINDOWS):
        cs_g = cs[:, :, g]
        lo = jnp.maximum(t + 1 - w, 0)
        win_sum = cs_g[:, 1:] - jnp.take(cs_g, lo, axis=1)
        count = jnp.minimum(t + 1, w).astype(jnp.float32)
        means.append(win_sum / count[None, :, None])
    pooled = (jnp.stack(means, axis=2) - pf).astype(p.dtype)
    mixed = jnp.einsum('bsgc,gcd->bsgd', pooled, w_group).reshape(bsz, s, POOL_WIDTH)
    return mixed * scale


def gmlp_mixer(uv, ln_g, ln_b, w_s, b_s):
    zz = jax.nn.gelu(uv, approximate=False)
    u, v = jnp.split(zz, 2, axis=-1)
    v = layer_norm(v, ln_g, ln_b)
    bsz, s, _ = v.shape
    nc = s // GMLP_CHUNK
    vh = v.reshape(bsz, nc, GMLP_CHUNK, GMLP_HEADS, GMLP_HEAD_DIM)
    causal = jnp.tril(jnp.ones((GMLP_CHUNK, GMLP_CHUNK), dtype=bool))
    w_masked = jnp.where(causal[None], w_s, jnp.zeros_like(w_s))
    mixed = jnp.einsum('hts,bnshd->bnthd', w_masked, vh) + b_s.T[:, :, None]
    return u * mixed.reshape(bsz, s, GMLP_WIDTH)


def causal_depthwise_conv(x, w, bias):
    out = lax.conv_general_dilated(x, w[:, None, :], window_strides=(1,),
                                   padding=[(SSM_CONV - 1, 0)],
                                   dimension_numbers=('NWC', 'WIO', 'NWC'),
                                   feature_group_count=x.shape[-1])
    return out + bias


def ssd_chunked(x, dt, a, b_mat, c_mat):
    bsz, s, h, p = x.shape
    L = SSM_CHUNK
    nc = s // L
    G, R, N = SSM_GROUPS, h // SSM_GROUPS, SSM_STATE
    xd = (x * dt[..., None]).reshape(bsz, nc, L, G, R, p)
    da = (dt * a).reshape(bsz, nc, L, h).transpose(0, 3, 1, 2)
    a_cs = jnp.cumsum(da, axis=-1)
    bc = b_mat.reshape(bsz, nc, L, G, N)
    cc = c_mat.reshape(bsz, nc, L, G, N)
    causal = jnp.tril(jnp.ones((L, L), dtype=bool))
    diff = a_cs[..., :, None] - a_cs[..., None, :]
    decay = jnp.exp(jnp.where(causal, diff, -jnp.inf)).reshape(bsz, G, R, nc, L, L)
    cb = jnp.einsum('bclgn,bcsgn->bgcls', cc, bc)
    y_diag = jnp.einsum('bgrcls,bcsgrp->bclgrp', cb[:, :, None] * decay, xd)
    decay_states = jnp.exp(a_cs[..., -1:] - a_cs).reshape(bsz, G, R, nc, L)
    states = jnp.einsum('bclgn,bgrcl,bclgrp->bcgrpn', bc, decay_states, xd)
    chunk_decay = jnp.exp(a_cs[..., -1]).reshape(bsz, G, R, nc)

    def step(state, inp):
        st, dec = inp
        return state * dec[..., None, None] + st, state

    h0 = jnp.zeros((bsz, G, R, p, N), jnp.float32)
    _, prev = lax.scan(step, h0, (jnp.moveaxis(states, 1, 0), jnp.moveaxis(chunk_decay, -1, 0)))
    prev = jnp.moveaxis(prev, 0, 1)
    decay_out = jnp.exp(a_cs).reshape(bsz, G, R, nc, L)
    y_off = jnp.einsum('bclgn,bcgrpn,bgrcl->bclgrp', cc, prev, decay_out)
    return (y_diag + y_off).reshape(bsz, s, h, p)


def ssm_mixer(z, xbc, dt_raw, conv_w, conv_b, dt_bias, a_log, d_skip, norm_g):
    bsz, s, _ = z.shape
    xbc = jax.nn.silu(causal_depthwise_conv(xbc, conv_w, conv_b)).astype(jnp.float32)
    xs, bm, cm = jnp.split(xbc, (SSM_INNER, SSM_INNER + SSM_GROUPS * SSM_STATE), axis=-1)
    xs = xs.reshape(bsz, s, SSM_HEADS, SSM_HEAD_DIM)
    bm = bm.reshape(bsz, s, SSM_GROUPS, SSM_STATE)
    cm = cm.reshape(bsz, s, SSM_GROUPS, SSM_STATE)
    dt = jax.nn.softplus(dt_raw.astype(jnp.float32) + dt_bias.astype(jnp.float32))
    a = -jnp.exp(a_log.astype(jnp.float32))
    y = ssd_chunked(xs, dt, a, bm, cm) + d_skip.astype(jnp.float32)[:, None] * xs
    y = y.reshape(bsz, s, SSM_INNER) * jax.nn.silu(z.astype(jnp.float32))
    yg = y.reshape(bsz, s, SSM_GROUPS, SSM_INNER // SSM_GROUPS)
    yg = yg * lax.rsqrt(jnp.mean(yg * yg, axis=-1, keepdims=True) + RMS_EPS)
    return (yg.reshape(bsz, s, SSM_INNER) * norm_g.astype(jnp.float32)).astype(z.dtype)


def setup_inputs(seed: int = 0) -> dict:
    key = jax.random.key(seed)
    ks = jax.random.split(key, 32)
    f32 = jnp.float32
    nrm = lambda k, shape, scale: jax.random.normal(k, shape, f32) * scale
    dt0 = jnp.exp(jax.random.uniform(ks[12], (DEPTH, SSM_HEADS), f32, np.log(1e-3), np.log(1e-1)))
    return {
        'x': nrm(ks[0], (BATCH, SEQ, D_MODEL), 1.0),
        'c': nrm(ks[1], (BATCH, D_MODEL), 1.0),
        'w_ada': nrm(ks[2], (DEPTH, D_MODEL, N_ADA * D_MODEL), D_MODEL ** -0.5),
        'b_ada': nrm(ks[3], (DEPTH, N_ADA * D_MODEL), 0.02),
        'w_in': nrm(ks[4], (DEPTH, D_MODEL, IN_PROJ_DIM), D_MODEL ** -0.5),
        'pool_w': nrm(ks[5], (DEPTH, POOL_GROUPS, POOL_GROUP_DIM, POOL_GROUP_DIM), POOL_GROUP_DIM ** -0.5),
        'pool_scale': 1.0 + nrm(ks[6], (DEPTH, POOL_WIDTH), 0.02),
        'gmlp_ln_g': 1.0 + nrm(ks[7], (DEPTH, GMLP_WIDTH), 0.02),
        'gmlp_ln_b': nrm(ks[8], (DEPTH, GMLP_WIDTH), 0.02),
        'gmlp_ws': nrm(ks[9], (DEPTH, GMLP_HEADS, GMLP_CHUNK, GMLP_CHUNK), GMLP_CHUNK ** -0.5),
        'gmlp_bs': 1.0 + nrm(ks[10], (DEPTH, GMLP_HEADS, GMLP_CHUNK), 0.02),
        'conv_w': nrm(ks[11], (DEPTH, SSM_CONV, SSM_CONV_DIM), SSM_CONV ** -0.5),
        'conv_b': nrm(ks[13], (DEPTH, SSM_CONV_DIM), 0.02),
        'dt_bias': dt0 + jnp.log(-jnp.expm1(-dt0)),
        'a_log': jnp.log(jax.random.uniform(ks[14], (DEPTH, SSM_HEADS), f32, 1.0, 16.0)),
        'd_skip': 1.0 + nrm(ks[15], (DEPTH, SSM_HEADS), 0.02),
        'ssm_norm': 1.0 + nrm(ks[16], (DEPTH, SSM_INNER), 0.02),
        'w_pool_out': nrm(ks[17], (DEPTH, POOL_WIDTH, D_MODEL), POOL_WIDTH ** -0.5),
        'w_gmlp_out': nrm(ks[18], (DEPTH, GMLP_WIDTH, D_MODEL), GMLP_WIDTH ** -0.5),
        'w_ssm_out': nrm(ks[19], (DEPTH, SSM_INNER, D_MODEL), SSM_INNER ** -0.5),
        'w_o': nrm(ks[20], (DEPTH, D_MODEL, D_MODEL), D_MODEL ** -0.5),
        'w_up': nrm(ks[21], (DEPTH, D_MODEL, D_FF), D_MODEL ** -0.5),
        'w_down': nrm(ks[22], (DEPTH, D_FF, D_MODEL), D_FF ** -0.5),
        'final_norm': 1.0 + nrm(ks[23], (D_MODEL,), 0.02),
    }


def reference(x, c, w_ada, b_ada, w_in, pool_w, pool_scale, gmlp_ln_g, gmlp_ln_b, gmlp_ws, gmlp_bs,
              conv_w, conv_b, dt_bias, a_log, d_skip, ssm_norm, w_pool_out, w_gmlp_out, w_ssm_out,
              w_o, w_up, w_down, final_norm):
    c_act = jax.nn.silu(c)
    for l in range(DEPTH):
        ada = (c_act @ w_ada[l] + b_ada[l])[:, None, :]
        sh1, sc1, g1, sh2, sc2, g2 = jnp.split(ada, N_ADA, axis=-1)
        h = rms_norm(x) * (1.0 + sc1) + sh1
        proj = h @ w_in[l]
        p, uv, z, xbc, dt_raw, gates = jnp.split(proj, IN_SPLITS, axis=-1)
        y_a = pool_mixer(p, pool_w[l], pool_scale[l]) @ w_pool_out[l]
        y_b = gmlp_mixer(uv, gmlp_ln_g[l], gmlp_ln_b[l], gmlp_ws[l], gmlp_bs[l]) @ w_gmlp_out[l]
        y_c = ssm_mixer(z, xbc, dt_raw, conv_w[l], conv_b[l], dt_bias[l], a_log[l], d_skip[l],
                        ssm_norm[l]) @ w_ssm_out[l]
        ga, gb, gc = jnp.split(jax.nn.sigmoid(gates), N_BRANCHES, axis=-1)
        merged = ga * y_a + gb * y_b + gc * y_c
        x = x + g1 * (merged @ w_o[l])
        h = rms_norm(x) * (1.0 + sc2) + sh2
        x = x + g2 * (jnp.square(jax.nn.relu(h @ w_up[l])) @ w_down[l])
    return rms_norm(x) * final_norm
```

```python
import functools

import jax
import jax.numpy as jnp
from jax import lax
from jax.experimental import pallas as pl
from jax.experimental.pallas import tpu as pltpu

f32 = jnp.float32
bf16 = jnp.bfloat16

D_MODEL = 2048
SEQ = 2048
DEPTH = 2
POOL_WIDTH = 1024
POOL_WINDOWS = (2, 4, 8, 16)
POOL_GROUP_DIM = 256
POOL_HALO = 16
GMLP_WIDTH = 1024
GMLP_HEADS = 8
GMLP_HEAD_DIM = 128
CHUNK = 128
SSM_INNER = 2048
SSM_HEAD_DIM = 64
SSM_HEADS = 32
SSM_GROUPS = 8
SSM_STATE = 128
SSM_CONV = 4
CONV_HALO = 8
SSM_CONV_DIM = 4096
GROUP_COLS = SSM_INNER // SSM_GROUPS
N_ADA = 6
D_FF = 8192
RMS_EPS = 1e-6
LN_EPS = 1e-5
LANES = 128
DT_PAD = LANES

XBC_OFF = 0
GATE_OFF = 4096
UV_OFF = 10240
Z_OFF = 12288
P_OFF = 14336
N_MAIN = 15360

VMEM_LIMIT = 56 * 1024 * 1024

SH1, SC1, G1, SH2, SC2, G2 = range(N_ADA)


def _params(*sem):
    return pltpu.CompilerParams(dimension_semantics=sem, vmem_limit_bytes=VMEM_LIMIT)


def _split_bf16(x, n):
    parts, r = [], x
    for _ in range(n):
        p = r.astype(bf16)
        parts.append(p)
        r = r - p.astype(f32)
    return parts


def _mod_rmsnorm(x_ref, sh_ref, sc_ref, h_ref, rows=256):
    sh = sh_ref[0]
    sc1p = 1.0 + sc_ref[0]

    def body(r, carry):
        rs = pl.ds(pl.multiple_of(r * rows, rows), rows)
        xf = x_ref[rs, :]
        ms = jnp.mean(xf * xf, axis=-1, keepdims=True)
        h_ref[rs, :] = ((xf * lax.rsqrt(ms + RMS_EPS)) * sc1p + sh).astype(h_ref.dtype)
        return carry

    lax.fori_loop(0, x_ref.shape[0] // rows, body, 0)


def _ada_kernel(c_ref, w_ref, b_ref, o_ref):
    c_act = jax.nn.silu(c_ref[...]).astype(bf16)
    o_ref[0] = jnp.dot(c_act, w_ref[0].astype(bf16), preferred_element_type=f32) + b_ref[0]


def _ada(c, w_ada, b_ada, tn=1024):
    depth, d, n = w_ada.shape
    bsz = c.shape[0]
    return pl.pallas_call(
        _ada_kernel,
        grid=(depth, n // tn),
        in_specs=[
            pl.BlockSpec((bsz, d), lambda l, j: (0, 0)),
            pl.BlockSpec((1, d, tn), lambda l, j: (l, 0, j)),
            pl.BlockSpec((1, 1, tn), lambda l, j: (l, 0, j)),
        ],
        out_specs=pl.BlockSpec((1, bsz, tn), lambda l, j: (l, 0, j)),
        out_shape=jax.ShapeDtypeStruct((depth, bsz, n), f32),
        compiler_params=_params("parallel", "parallel"),
        name="ada",
    )(c, w_ada, b_ada.reshape(depth, 1, n))


def _inproj_kernel(x_ref, sh_ref, sc_ref, w_ref, wdt_ref, o_ref, odt_ref, h_ref):
    @pl.when(pl.program_id(1) == 0)
    def _():
        _mod_rmsnorm(x_ref, sh_ref, sc_ref, h_ref)
        odt_ref[...] = jnp.dot(h_ref[...], wdt_ref[...], preferred_element_type=f32)

    o_ref[...] = jnp.dot(h_ref[...], w_ref[...], preferred_element_type=f32)


def _inproj(xf, mod, w_main, w_dt, tm=1024, tn=512):
    t, d = xf.shape
    tiles_per_batch = SEQ // tm
    return pl.pallas_call(
        _inproj_kernel,
        grid=(t // tm, N_MAIN // tn),
        in_specs=[
            pl.BlockSpec((tm, d), lambda i, j: (i, 0)),
            pl.BlockSpec((1, 1, d), lambda i, j: ((i // tiles_per_batch) * N_ADA + SH1, 0, 0)),
            pl.BlockSpec((1, 1, d), lambda i, j: ((i // tiles_per_batch) * N_ADA + SC1, 0, 0)),
            pl.BlockSpec((d, tn), lambda i, j: (0, j)),
            pl.BlockSpec((d, DT_PAD), lambda i, j: (0, 0)),
        ],
        out_specs=[
            pl.BlockSpec((tm, tn), lambda i, j: (i, j)),
            pl.BlockSpec((tm, DT_PAD), lambda i, j: (i, 0)),
        ],
        out_shape=[
            jax.ShapeDtypeStruct((t, N_MAIN), f32),
            jax.ShapeDtypeStruct((t, DT_PAD), f32),
        ],
        scratch_shapes=[pltpu.VMEM((tm, d), bf16)],
        compiler_params=_params("parallel", "arbitrary"),
        name="inproj",
    )(xf, mod, mod, w_main, w_dt)


def _pool_kernel(p_ref, halo_ref, pw_ref, scale_ref, o_ref, *, tiles_per_seq):
    ts = p_ref.shape[0]
    tile_in_seq = pl.program_id(0) % tiles_per_seq
    first = tile_in_seq == 0
    pos = lax.broadcasted_iota(jnp.int32, (ts, 1), 0) + tile_in_seq * ts
    for g, w in enumerate(POOL_WINDOWS):
        cols = slice(g * POOL_GROUP_DIM, (g + 1) * POOL_GROUP_DIM)
        pg = p_ref[:, cols]
        halo = jnp.where(first, 0.0, halo_ref[:, cols])
        s = jnp.concatenate([halo, pg], axis=0)
        k = 1
        while k < w:
            s = s + pltpu.roll(s, k, axis=0)
            k *= 2
        count = jnp.minimum(pos + 1, w).astype(f32)
        pooled = s[POOL_HALO:, :] / count - pg
        mixed = jnp.dot(pooled.astype(bf16), pw_ref[g], preferred_element_type=f32)
        o_ref[:, cols] = (mixed * scale_ref[:, cols]).astype(o_ref.dtype)


def _pool(proj, pool_w, pool_scale, ts=512):
    t = proj.shape[0]
    halo_blocks = ts // POOL_HALO
    p_blk = P_OFF // POOL_WIDTH
    return pl.pallas_call(
        functools.partial(_pool_kernel, tiles_per_seq=SEQ // ts),
        grid=(t // ts,),
        in_specs=[
            pl.BlockSpec((ts, POOL_WIDTH), lambda i: (i, p_blk)),
            pl.BlockSpec((POOL_HALO, POOL_WIDTH), lambda i: (jnp.maximum(i * halo_blocks - 1, 0), p_blk)),
            pl.BlockSpec(pool_w.shape, lambda i: (0, 0, 0)),
            pl.BlockSpec((1, POOL_WIDTH), lambda i: (0, 0)),
        ],
        out_specs=pl.BlockSpec((ts, POOL_WIDTH), lambda i: (i, 0)),
        out_shape=jax.ShapeDtypeStruct((t, POOL_WIDTH), bf16),
        compiler_params=_params("parallel"),
        name="pool",
    )(proj, proj, pool_w, pool_scale)


def _gelu(x):
    return 0.5 * x * (1.0 + lax.erf(x * 0.7071067811865476))


def _gmlp_kernel(u_ref, v_ref, g_ref, b_ref, ws_ref, bs_ref, o_ref, vn_ref):
    tg = u_ref.shape[0]
    v = _gelu(v_ref[...])
    mu = jnp.mean(v, axis=-1, keepdims=True)
    dv = v - mu
    var = jnp.mean(dv * dv, axis=-1, keepdims=True)
    vn_ref[...] = ((dv * lax.rsqrt(var + LN_EPS)) * g_ref[...] + b_ref[...]).astype(vn_ref.dtype)
    ri = lax.broadcasted_iota(jnp.int32, (CHUNK, CHUNK), 0)
    ci = lax.broadcasted_iota(jnp.int32, (CHUNK, CHUNK), 1)
    causal = ri >= ci
    for h in range(GMLP_HEADS):
        cols = slice(h * GMLP_HEAD_DIM, (h + 1) * GMLP_HEAD_DIM)
        wm = jnp.where(causal, ws_ref[h], 0.0).astype(bf16)
        bias = bs_ref[h]
        for c in range(tg // CHUNK):
            rows = slice(c * CHUNK, (c + 1) * CHUNK)
            mixed = jnp.dot(wm, vn_ref[rows, cols], preferred_element_type=f32) + bias
            o_ref[rows, cols] = (_gelu(u_ref[rows, cols]) * mixed).astype(o_ref.dtype)


def _gmlp(proj, ln_g, ln_b, ws, bs, tg=512):
    t = proj.shape[0]
    u_blk = UV_OFF // GMLP_WIDTH
    return pl.pallas_call(
        _gmlp_kernel,
        grid=(t // tg,),
        in_specs=[
            pl.BlockSpec((tg, GMLP_WIDTH), lambda i: (i, u_blk)),
            pl.BlockSpec((tg, GMLP_WIDTH), lambda i: (i, u_blk + 1)),
            pl.BlockSpec((1, GMLP_WIDTH), lambda i: (0, 0)),
            pl.BlockSpec((1, GMLP_WIDTH), lambda i: (0, 0)),
            pl.BlockSpec((GMLP_HEADS, CHUNK, CHUNK), lambda i: (0, 0, 0)),
            pl.BlockSpec((GMLP_HEADS, CHUNK, 1), lambda i: (0, 0, 0)),
        ],
        out_specs=pl.BlockSpec((tg, GMLP_WIDTH), lambda i: (i, 0)),
        out_shape=jax.ShapeDtypeStruct((t, GMLP_WIDTH), bf16),
        scratch_shapes=[pltpu.VMEM((tg, GMLP_WIDTH), bf16)],
        compiler_params=_params("parallel"),
        name="gmlp",
    )(proj, proj, ln_g, ln_b, ws, bs)


def _ssd_kernel(xbc_ref, halo_ref, z_ref, dtr_ref, cw_ref, cb_ref, dtb_ref, alog_ref, dskip_ref,
                ng_ref, o_ref, state_ref, xs_ref, bm_ref, cm_ref, wdt_ref, eout_ref):
    chunk = pl.program_id(1)

    @pl.when(chunk == 0)
    def _():
        state_ref[...] = jnp.zeros_like(state_ref)

    slab = 512
    for s in range(SSM_CONV_DIM // slab):
        cols = slice(s * slab, (s + 1) * slab)
        halo = jnp.where(chunk == 0, 0.0, halo_ref[:, cols])
        e = jnp.concatenate([halo, xbc_ref[:, cols]], axis=0)
        w = cw_ref[:, cols]
        acc = e[CONV_HALO:, :] * w[SSM_CONV - 1:SSM_CONV, :]
        for k in range(1, SSM_CONV):
            acc = acc + pltpu.roll(e, k, axis=0)[CONV_HALO:, :] * w[SSM_CONV - 1 - k:SSM_CONV - k, :]
        act = jax.nn.silu(acc + cb_ref[:, cols])
        if s * slab < SSM_INNER:
            xs_ref[:, cols] = act
        elif s * slab < SSM_INNER + SSM_GROUPS * SSM_STATE:
            off = s * slab - SSM_INNER
            bm_ref[:, off:off + slab] = act.astype(bf16)
        else:
            off = s * slab - SSM_INNER - SSM_GROUPS * SSM_STATE
            cm_ref[:, off:off + slab] = act.astype(bf16)

    dt = jax.nn.softplus(dtr_ref[...] + dtb_ref[...])
    da = dt * (-jnp.exp(alog_ref[...]))
    ri = lax.broadcasted_iota(jnp.int32, (CHUNK, CHUNK), 0)
    ci = lax.broadcasted_iota(jnp.int32, (CHUNK, CHUNK), 1)
    causal = ri >= ci
    tril = jnp.where(causal, 1.0, 0.0).astype(bf16)
    a_cs = sum(jnp.dot(tril, p, preferred_element_type=f32) for p in _split_bf16(da, 3))
    a_last = a_cs[CHUNK - 1:CHUNK, :]
    wdt = dt * jnp.exp(a_last - a_cs)
    e_out = jnp.exp(a_cs)
    a_cs_t = a_cs.T
    dt_t = dt.T
    hi = lax.broadcasted_iota(jnp.int32, (LANES, SSM_INNER), 0)
    ch = lax.broadcasted_iota(jnp.int32, (LANES, SSM_INNER), 1)
    expand = jnp.where(hi == ch // SSM_HEAD_DIM, 1.0, 0.0).astype(bf16)
    wdt_ref[...] = sum(jnp.dot(p, expand, preferred_element_type=f32) for p in _split_bf16(wdt, 3))
    eout_ref[...] = sum(jnp.dot(p, expand, preferred_element_type=f32) for p in _split_bf16(e_out, 3))

    lane = lax.broadcasted_iota(jnp.int32, (CHUNK, LANES), 1)
    low_half = lane < SSM_HEAD_DIM
    for g in range(SSM_GROUPS):
        gcols = slice(g * GROUP_COLS, (g + 1) * GROUP_COLS)
        ncols = slice(g * SSM_STATE, (g + 1) * SSM_STATE)
        bg = bm_ref[:, ncols]
        cg = cm_ref[:, ncols]
        cb = lax.dot_general(cg, bg, (((1,), (1,)), ((), ())), preferred_element_type=f32)
        y_pairs = []
        for pr in range(2):
            scores = []
            for hh in range(2):
                h = 4 * g + 2 * pr + hh
                diff = a_cs[:, h:h + 1] - a_cs_t[h:h + 1, :]
                decay = jnp.exp(jnp.where(causal, diff, -jnp.inf))
                scores.append((cb * decay * dt_t[h:h + 1, :]).astype(bf16))
            lhs = jnp.concatenate(scores, axis=1)
            xp = xs_ref[:, g * GROUP_COLS + pr * LANES:g * GROUP_COLS + (pr + 1) * LANES]
            rhs = jnp.concatenate([jnp.where(low_half, xp, 0.0), jnp.where(low_half, 0.0, xp)], axis=0)
            y_pairs.append(jnp.dot(lhs, rhs.astype(bf16), preferred_element_type=f32))
        y = jnp.concatenate(y_pairs, axis=1)
        xs_g = xs_ref[:, gcols]
        xdw = (xs_g * wdt_ref[:, gcols]).astype(bf16)
        st_new = lax.dot_general(bg, xdw, (((0,), (0,)), ((), ())), preferred_element_type=f32)
        prev = state_ref[g]
        y = y + jnp.dot(cg, prev.astype(bf16), preferred_element_type=f32) * eout_ref[:, gcols]
        state_ref[g] = prev * eout_ref[CHUNK - 1:CHUNK, gcols] + st_new
        y = y + dskip_ref[:, gcols] * xs_g
        y = y * jax.nn.silu(z_ref[:, gcols])
        y = y * lax.rsqrt(jnp.mean(y * y, axis=-1, keepdims=True) + RMS_EPS)
        o_ref[:, gcols] = (y * ng_ref[:, gcols]).astype(o_ref.dtype)


def _ssd(proj, dtr, conv_w, conv_b, dt_bias, a_log, d_skip, norm_g, bsz):
    t = proj.shape[0]
    nc = SEQ // CHUNK
    halo_blocks = CHUNK // CONV_HALO
    row = lambda b, c: b * nc + c
    const = lambda b, c: (0, 0)
    return pl.pallas_call(
        _ssd_kernel,
        grid=(bsz, nc),
        in_specs=[
            pl.BlockSpec((CHUNK, SSM_CONV_DIM), lambda b, c: (row(b, c), XBC_OFF // SSM_CONV_DIM)),
            pl.BlockSpec((CONV_HALO, SSM_CONV_DIM),
                         lambda b, c: (jnp.maximum(row(b, c) * halo_blocks - 1, 0), XBC_OFF // SSM_CONV_DIM)),
            pl.BlockSpec((CHUNK, SSM_INNER), lambda b, c: (row(b, c), Z_OFF // SSM_INNER)),
            pl.BlockSpec((CHUNK, DT_PAD), lambda b, c: (row(b, c), 0)),
            pl.BlockSpec((SSM_CONV, SSM_CONV_DIM), const),
            pl.BlockSpec((1, SSM_CONV_DIM), const),
            pl.BlockSpec((1, DT_PAD), const),
            pl.BlockSpec((1, DT_PAD), const),
            pl.BlockSpec((1, SSM_INNER), const),
            pl.BlockSpec((1, SSM_INNER), const),
        ],
        out_specs=pl.BlockSpec((CHUNK, SSM_INNER), lambda b, c: (row(b, c), 0)),
        out_shape=jax.ShapeDtypeStruct((t, SSM_INNER), bf16),
        scratch_shapes=[
            pltpu.VMEM((SSM_GROUPS, SSM_STATE, GROUP_COLS), f32),
            pltpu.VMEM((CHUNK, SSM_INNER), f32),
            pltpu.VMEM((CHUNK, SSM_GROUPS * SSM_STATE), bf16),
            pltpu.VMEM((CHUNK, SSM_GROUPS * SSM_STATE), bf16),
            pltpu.VMEM((CHUNK, SSM_INNER), f32),
            pltpu.VMEM((CHUNK, SSM_INNER), f32),
        ],
        compiler_params=_params("parallel", "arbitrary"),
        name="ssd",
    )(proj, proj, proj, dtr, conv_w, conv_b, dt_bias, a_log, d_skip, norm_g)


def _merge_kernel(a_ref, b_ref, c_ref, wa_ref, wb_ref, wc_ref, ga_ref, gb_ref, gc_ref, o_ref):
    ya = jnp.dot(a_ref[...], wa_ref[...], preferred_element_type=f32)
    yb = jnp.dot(b_ref[...], wb_ref[...], preferred_element_type=f32)
    yc = jnp.dot(c_ref[...], wc_ref[...], preferred_element_type=f32)
    merged = (jax.nn.sigmoid(ga_ref[...]) * ya + jax.nn.sigmoid(gb_ref[...]) * yb
              + jax.nn.sigmoid(gc_ref[...]) * yc)
    o_ref[...] = merged.astype(o_ref.dtype)


def _merge(ma, gb_, yn, proj, wa, wb, wc, tm=1024, tn=512):
    t = proj.shape[0]
    d = D_MODEL
    gate_blk = GATE_OFF // tn
    per_gate = d // tn
    return pl.pallas_call(
        _merge_kernel,
        grid=(t // tm, d // tn),
        in_specs=[
            pl.BlockSpec((tm, POOL_WIDTH), lambda i, j: (i, 0)),
            pl.BlockSpec((tm, GMLP_WIDTH), lambda i, j: (i, 0)),
            pl.BlockSpec((tm, SSM_INNER), lambda i, j: (i, 0)),
            pl.BlockSpec((POOL_WIDTH, tn), lambda i, j: (0, j)),
            pl.BlockSpec((GMLP_WIDTH, tn), lambda i, j: (0, j)),
            pl.BlockSpec((SSM_INNER, tn), lambda i, j: (0, j)),
            pl.BlockSpec((tm, tn), lambda i, j: (i, gate_blk + j)),
            pl.BlockSpec((tm, tn), lambda i, j: (i, gate_blk + per_gate + j)),
            pl.BlockSpec((tm, tn), lambda i, j: (i, gate_blk + 2 * per_gate + j)),
        ],
        out_specs=pl.BlockSpec((tm, tn), lambda i, j: (i, j)),
        out_shape=jax.ShapeDtypeStruct((t, d), bf16),
        compiler_params=_params("parallel", "arbitrary"),
        name="merge",
    )(ma, gb_, yn, wa, wb, wc, proj, proj, proj)


def _wo_kernel(m_ref, w_ref, x_ref, g_ref, o_ref):
    o_ref[...] = x_ref[...] + g_ref[0] * jnp.dot(m_ref[...], w_ref[...], preferred_element_type=f32)


def _wo(merged, w_o, xf, mod, tm=1024, tn=512):
    t, d = xf.shape
    tiles_per_batch = SEQ // tm
    return pl.pallas_call(
        _wo_kernel,
        grid=(t // tm, d // tn),
        in_specs=[
            pl.BlockSpec((tm, d), lambda i, j: (i, 0)),
            pl.BlockSpec((d, tn), lambda i, j: (0, j)),
            pl.BlockSpec((tm, tn), lambda i, j: (i, j)),
            pl.BlockSpec((1, 1, tn), lambda i, j: ((i // tiles_per_batch) * N_ADA + G1, 0, j)),
        ],
        out_specs=pl.BlockSpec((tm, tn), lambda i, j: (i, j)),
        out_shape=jax.ShapeDtypeStruct((t, d), f32),
        compiler_params=_params("parallel", "arbitrary"),
        name="wo",
    )(merged, w_o, xf, mod)


def _mlp_kernel(x_ref, sh_ref, sc_ref, g_ref, wu_ref, wd_ref, fn_ref, o_ref, h_ref, acc_ref, *, final_norm):
    f = pl.program_id(1)

    @pl.when(f == 0)
    def _():
        _mod_rmsnorm(x_ref, sh_ref, sc_ref, h_ref)

    u = jnp.dot(h_ref[...], wu_ref[...], preferred_element_type=f32)
    u = jnp.square(jnp.maximum(u, 0.0)).astype(bf16)
    part = jnp.dot(u, wd_ref[...], preferred_element_type=f32)

    @pl.when(f == 0)
    def _():
        acc_ref[...] = part

    @pl.when(f > 0)
    def _():
        acc_ref[...] += part

    @pl.when(f == pl.num_programs(1) - 1)
    def _():
        rows = 256
        gate = g_ref[0]

        def body(r, carry):
            rs = pl.ds(pl.multiple_of(r * rows, rows), rows)
            xn = x_ref[rs, :] + gate * acc_ref[rs, :]
            if final_norm:
                ms = jnp.mean(xn * xn, axis=-1, keepdims=True)
                xn = (xn * lax.rsqrt(ms + RMS_EPS)) * fn_ref[...]
            o_ref[rs, :] = xn
            return carry

        lax.fori_loop(0, x_ref.shape[0] // rows, body, 0)


def _mlp(xf, mod, w_up, w_down, fn, final_norm, tm=512, tf=512):
    t, d = xf.shape
    tiles_per_batch = SEQ // tm
    mod_spec = lambda k: pl.BlockSpec((1, 1, d), lambda i, f: ((i // tiles_per_batch) * N_ADA + k, 0, 0))
    return pl.pallas_call(
        functools.partial(_mlp_kernel, final_norm=final_norm),
        grid=(t // tm, D_FF // tf),
        in_specs=[
            pl.BlockSpec((tm, d), lambda i, f: (i, 0)),
            mod_spec(SH2),
            mod_spec(SC2),
            mod_spec(G2),
            pl.BlockSpec((d, tf), lambda i, f: (0, f)),
            pl.BlockSpec((tf, d), lambda i, f: (f, 0)),
            pl.BlockSpec((1, d), lambda i, f: (0, 0)),
        ],
        out_specs=pl.BlockSpec((tm, d), lambda i, f: (i, 0)),
        out_shape=jax.ShapeDtypeStruct((t, d), f32),
        scratch_shapes=[pltpu.VMEM((tm, d), bf16), pltpu.VMEM((tm, d), f32)],
        compiler_params=_params("parallel", "arbitrary"),
        name="mlp",
    )(xf, mod, mod, mod, w_up, w_down, fn)


def _pad_lanes(v, fill=0.0):
    return jnp.pad(v.astype(f32), (0, DT_PAD - v.shape[0]), constant_values=fill).reshape(1, DT_PAD)


def kernel(x, c, w_ada, b_ada, w_in, pool_w, pool_scale, gmlp_ln_g, gmlp_ln_b, gmlp_ws, gmlp_bs, conv_w, conv_b, dt_bias, a_log, d_skip, ssm_norm, w_pool_out, w_gmlp_out, w_ssm_out, w_o, w_up, w_down, final_norm):
    bsz, seq, d = x.shape
    assert (seq, d) == (SEQ, D_MODEL) and w_ada.shape[0] == DEPTH
    xf = x.reshape(bsz * seq, d)
    ada = _ada(c, w_ada, b_ada)
    fn = final_norm.reshape(1, d)
    o_uv, o_z, o_xbc, o_dt, o_gate = 1024, 3072, 5120, 9216, 9248
    for l in range(DEPTH):
        mod = ada[l].reshape(bsz * N_ADA, 1, d)
        wl = w_in[l]
        w_main = jnp.concatenate(
            [wl[:, o_xbc:o_dt], wl[:, o_gate:], wl[:, o_uv:o_z], wl[:, o_z:o_xbc], wl[:, :o_uv]], axis=1
        ).astype(bf16)
        w_dt = jnp.pad(wl[:, o_dt:o_gate], ((0, 0), (0, DT_PAD - SSM_HEADS))).astype(bf16)
        proj, dtr = _inproj(xf, mod, w_main, w_dt)
        mixed_a = _pool(proj, pool_w[l].astype(bf16), pool_scale[l].reshape(1, POOL_WIDTH))
        gated_b = _gmlp(proj, gmlp_ln_g[l].reshape(1, GMLP_WIDTH), gmlp_ln_b[l].reshape(1, GMLP_WIDTH),
                        gmlp_ws[l], gmlp_bs[l].reshape(GMLP_HEADS, CHUNK, 1))
        yn = _ssd(proj, dtr, conv_w[l], conv_b[l].reshape(1, SSM_CONV_DIM), _pad_lanes(dt_bias[l]),
                  _pad_lanes(a_log[l]), jnp.repeat(d_skip[l], SSM_HEAD_DIM).reshape(1, SSM_INNER),
                  ssm_norm[l].reshape(1, SSM_INNER), bsz)
        merged = _merge(mixed_a, gated_b, yn, proj, w_pool_out[l].astype(bf16), w_gmlp_out[l].astype(bf16),
                        w_ssm_out[l].astype(bf16))
        xf = _wo(merged, w_o[l].astype(bf16), xf, mod)
        xf = _mlp(xf, mod, w_up[l].astype(bf16), w_down[l].astype(bf16), fn, final_norm=(l == DEPTH - 1))
    return xf.reshape(bsz, seq, d)
```

```python
import functools

import jax
import jax.numpy as jnp
from jax import lax
from jax.experimental import pallas as pl
from jax.experimental.pallas import tpu as pltpu

f32 = jnp.float32
bf16 = jnp.bfloat16

D_MODEL = 2048
SEQ = 2048
DEPTH = 2
POOL_WIDTH = 1024
POOL_WINDOWS = (2, 4, 8, 16)
POOL_GROUP_DIM = 256
POOL_HALO = 16
GMLP_WIDTH = 1024
GMLP_HEADS = 8
GMLP_HEAD_DIM = 128
CHUNK = 128
SSM_INNER = 2048
SSM_HEAD_DIM = 64
SSM_HEADS = 32
SSM_GROUPS = 8
SSM_STATE = 128
SSM_CONV = 4
CONV_HALO = 8
SSM_CONV_DIM = 4096
GROUP_COLS = SSM_INNER // SSM_GROUPS
N_ADA = 6
D_FF = 8192
RMS_EPS = 1e-6
LN_EPS = 1e-5
LANES = 128
DT_PAD = LANES

XBC_OFF = 0
GATE_OFF = 4096
UV_OFF = 10240
Z_OFF = 12288
P_OFF = 14336
N_MAIN = 15360

VMEM_LIMIT = 56 * 1024 * 1024

SH1, SC1, G1, SH2, SC2, G2 = range(N_ADA)


def _params(*sem):
    return pltpu.CompilerParams(dimension_semantics=sem, vmem_limit_bytes=VMEM_LIMIT)


def _split_bf16(x, n):
    parts, r = [], x
    for _ in range(n):
        p = r.astype(bf16)
        parts.append(p)
        r = r - p.astype(f32)
    return parts


def _mod_rmsnorm(x_ref, sh_ref, sc_ref, h_ref, rows=256):
    sh = sh_ref[0]
    sc1p = 1.0 + sc_ref[0]

    def body(r, carry):
        rs = pl.ds(pl.multiple_of(r * rows, rows), rows)
        xf = x_ref[rs, :]
        ms = jnp.mean(xf * xf, axis=-1, keepdims=True)
        h_ref[rs, :] = ((xf * lax.rsqrt(ms + RMS_EPS)) * sc1p + sh).astype(h_ref.dtype)
        return carry

    lax.fori_loop(0, x_ref.shape[0] // rows, body, 0)


def _ada_kernel(c_ref, w_ref, b_ref, o_ref):
    c_act = jax.nn.silu(c_ref[...]).astype(bf16)
    o_ref[0] = jnp.dot(c_act, w_ref[0].astype(bf16), preferred_element_type=f32) + b_ref[0]


def _ada(c, w_ada, b_ada, tn=1024):
    depth, d, n = w_ada.shape
    bsz = c.shape[0]
    return pl.pallas_call(
        _ada_kernel,
        grid=(depth, n // tn),
        in_specs=[
            pl.BlockSpec((bsz, d), lambda l, j: (0, 0)),
            pl.BlockSpec((1, d, tn), lambda l, j: (l, 0, j)),
            pl.BlockSpec((1, 1, tn), lambda l, j: (l, 0, j)),
        ],
        out_specs=pl.BlockSpec((1, bsz, tn), lambda l, j: (l, 0, j)),
        out_shape=jax.ShapeDtypeStruct((depth, bsz, n), f32),
        compiler_params=_params("parallel", "parallel"),
        name="ada",
    )(c, w_ada, b_ada.reshape(depth, 1, n))


def _inproj_kernel(x_ref, sh_ref, sc_ref, w_ref, wdt_ref, o_ref, odt_ref, h_ref):
    @pl.when(pl.program_id(1) == 0)
    def _():
        _mod_rmsnorm(x_ref, sh_ref, sc_ref, h_ref)
        odt_ref[...] = jnp.dot(h_ref[...], wdt_ref[...], preferred_element_type=f32)

    o_ref[...] = jnp.dot(h_ref[...], w_ref[0], preferred_element_type=f32)


def _col_blocked(w, tn):
    k, n = w.shape
    return w.reshape(k, n // tn, tn).transpose(1, 0, 2)


def _inproj(xf, mod, w_main, w_dt, tm=1024):
    t, d = xf.shape
    tn = w_main.shape[2]
    tiles_per_batch = SEQ // tm
    return pl.pallas_call(
        _inproj_kernel,
        grid=(t // tm, N_MAIN // tn),
        in_specs=[
            pl.BlockSpec((tm, d), lambda i, j: (i, 0)),
            pl.BlockSpec((1, 1, d), lambda i, j: ((i // tiles_per_batch) * N_ADA + SH1, 0, 0)),
            pl.BlockSpec((1, 1, d), lambda i, j: ((i // tiles_per_batch) * N_ADA + SC1, 0, 0)),
            pl.BlockSpec((1, d, tn), lambda i, j: (j, 0, 0)),
            pl.BlockSpec((d, DT_PAD), lambda i, j: (0, 0)),
        ],
        out_specs=[
            pl.BlockSpec((tm, tn), lambda i, j: (i, j)),
            pl.BlockSpec((tm, DT_PAD), lambda i, j: (i, 0)),
        ],
        out_shape=[
            jax.ShapeDtypeStruct((t, N_MAIN), f32),
            jax.ShapeDtypeStruct((t, DT_PAD), f32),
        ],
        scratch_shapes=[pltpu.VMEM((tm, d), bf16)],
        compiler_params=_params("parallel", "arbitrary"),
        name="inproj",
    )(xf, mod, mod, w_main, w_dt)


def _pool_kernel(p_ref, halo_ref, pw_ref, scale_ref, o_ref, *, tiles_per_seq):
    ts = p_ref.shape[0]
    tile_in_seq = pl.program_id(0) % tiles_per_seq
    first = tile_in_seq == 0
    pos = lax.broadcasted_iota(jnp.int32, (ts, 1), 0) + tile_in_seq * ts
    for g, w in enumerate(POOL_WINDOWS):
        cols = slice(g * POOL_GROUP_DIM, (g + 1) * POOL_GROUP_DIM)
        pg = p_ref[:, cols]
        halo = jnp.where(first, 0.0, halo_ref[:, cols])
        s = jnp.concatenate([halo, pg], axis=0)
        k = 1
        while k < w:
            s = s + pltpu.roll(s, k, axis=0)
            k *= 2
        count = jnp.minimum(pos + 1, w).astype(f32)
        pooled = s[POOL_HALO:, :] / count - pg
        mixed = jnp.dot(pooled.astype(bf16), pw_ref[g], preferred_element_type=f32)
        o_ref[:, cols] = (mixed * scale_ref[:, cols]).astype(o_ref.dtype)


def _pool(proj, pool_w, pool_scale, ts=512):
    t = proj.shape[0]
    halo_blocks = ts // POOL_HALO
    p_blk = P_OFF // POOL_WIDTH
    return pl.pallas_call(
        functools.partial(_pool_kernel, tiles_per_seq=SEQ // ts),
        grid=(t // ts,),
        in_specs=[
            pl.BlockSpec((ts, POOL_WIDTH), lambda i: (i, p_blk)),
            pl.BlockSpec((POOL_HALO, POOL_WIDTH), lambda i: (jnp.maximum(i * halo_blocks - 1, 0), p_blk)),
            pl.BlockSpec(pool_w.shape, lambda i: (0, 0, 0)),
            pl.BlockSpec((1, POOL_WIDTH), lambda i: (0, 0)),
        ],
        out_specs=pl.BlockSpec((ts, POOL_WIDTH), lambda i: (i, 0)),
        out_shape=jax.ShapeDtypeStruct((t, POOL_WIDTH), bf16),
        compiler_params=_params("parallel"),
        name="pool",
    )(proj, proj, pool_w, pool_scale)


def _gelu(x):
    return 0.5 * x * (1.0 + lax.erf(x * 0.7071067811865476))


def _gmlp_kernel(u_ref, v_ref, g_ref, b_ref, ws_ref, bs_ref, o_ref, vn_ref):
    tg = u_ref.shape[0]
    v = _gelu(v_ref[...])
    mu = jnp.mean(v, axis=-1, keepdims=True)
    dv = v - mu
    var = jnp.mean(dv * dv, axis=-1, keepdims=True)
    vn_ref[...] = ((dv * lax.rsqrt(var + LN_EPS)) * g_ref[...] + b_ref[...]).astype(vn_ref.dtype)
    ri = lax.broadcasted_iota(jnp.int32, (CHUNK, CHUNK), 0)
    ci = lax.broadcasted_iota(jnp.int32, (CHUNK, CHUNK), 1)
    causal = ri >= ci
    for h in range(GMLP_HEADS):
        cols = slice(h * GMLP_HEAD_DIM, (h + 1) * GMLP_HEAD_DIM)
        wm = jnp.where(causal, ws_ref[h], 0.0).astype(bf16)
        bias = bs_ref[h]
        for c in range(tg // CHUNK):
            rows = slice(c * CHUNK, (c + 1) * CHUNK)
            mixed = jnp.dot(wm, vn_ref[rows, cols], preferred_element_type=f32) + bias
            o_ref[rows, cols] = (_gelu(u_ref[rows, cols]) * mixed).astype(o_ref.dtype)


def _gmlp(proj, ln_g, ln_b, ws, bs, tg=512):
    t = proj.shape[0]
    u_blk = UV_OFF // GMLP_WIDTH
    return pl.pallas_call(
        _gmlp_kernel,
        grid=(t // tg,),
        in_specs=[
            pl.BlockSpec((tg, GMLP_WIDTH), lambda i: (i, u_blk)),
            pl.BlockSpec((tg, GMLP_WIDTH), lambda i: (i, u_blk + 1)),
            pl.BlockSpec((1, GMLP_WIDTH), lambda i: (0, 0)),
            pl.BlockSpec((1, GMLP_WIDTH), lambda i: (0, 0)),
            pl.BlockSpec((GMLP_HEADS, CHUNK, CHUNK), lambda i: (0, 0, 0)),
            pl.BlockSpec((GMLP_HEADS, CHUNK, 1), lambda i: (0, 0, 0)),
        ],
        out_specs=pl.BlockSpec((tg, GMLP_WIDTH), lambda i: (i, 0)),
        out_shape=jax.ShapeDtypeStruct((t, GMLP_WIDTH), bf16),
        scratch_shapes=[pltpu.VMEM((tg, GMLP_WIDTH), bf16)],
        compiler_params=_params("parallel"),
        name="gmlp",
    )(proj, proj, ln_g, ln_b, ws, bs)


def _ssd_kernel(xbc_ref, halo_ref, z_ref, dtr_ref, cw_ref, cb_ref, dtb_ref, alog_ref, dskip_ref,
                ng_ref, o_ref, state_ref, xs_ref, bm_ref, cm_ref, wdt_ref, eout_ref):
    chunk = pl.program_id(1)

    @pl.when(chunk == 0)
    def _():
        state_ref[...] = jnp.zeros_like(state_ref)

    slab = 512
    for s in range(SSM_CONV_DIM // slab):
        cols = slice(s * slab, (s + 1) * slab)
        halo = jnp.where(chunk == 0, 0.0, halo_ref[:, cols])
        e = jnp.concatenate([halo, xbc_ref[:, cols]], axis=0)
        w = cw_ref[:, cols]
        acc = e[CONV_HALO:, :] * w[SSM_CONV - 1:SSM_CONV, :]
        for k in range(1, SSM_CONV):
            acc = acc + pltpu.roll(e, k, axis=0)[CONV_HALO:, :] * w[SSM_CONV - 1 - k:SSM_CONV - k, :]
        act = jax.nn.silu(acc + cb_ref[:, cols])
        if s * slab < SSM_INNER:
            xs_ref[:, cols] = act
        elif s * slab < SSM_INNER + SSM_GROUPS * SSM_STATE:
            off = s * slab - SSM_INNER
            bm_ref[:, off:off + slab] = act.astype(bf16)
        else:
            off = s * slab - SSM_INNER - SSM_GROUPS * SSM_STATE
            cm_ref[:, off:off + slab] = act.astype(bf16)

    dt = jax.nn.softplus(dtr_ref[...] + dtb_ref[...])
    da = dt * (-jnp.exp(alog_ref[...]))
    ri = lax.broadcasted_iota(jnp.int32, (CHUNK, CHUNK), 0)
    ci = lax.broadcasted_iota(jnp.int32, (CHUNK, CHUNK), 1)
    causal = ri >= ci
    tril = jnp.where(causal, 1.0, 0.0).astype(bf16)
    a_cs = sum(jnp.dot(tril, p, preferred_element_type=f32) for p in _split_bf16(da, 3))
    a_last = a_cs[CHUNK - 1:CHUNK, :]
    wdt = dt * jnp.exp(a_last - a_cs)
    e_out = jnp.exp(a_cs)
    a_cs_t = a_cs.T
    dt_t = dt.T
    hi = lax.broadcasted_iota(jnp.int32, (LANES, SSM_INNER), 0)
    ch = lax.broadcasted_iota(jnp.int32, (LANES, SSM_INNER), 1)
    expand = jnp.where(hi == ch // SSM_HEAD_DIM, 1.0, 0.0).astype(bf16)
    wdt_ref[...] = sum(jnp.dot(p, expand, preferred_element_type=f32) for p in _split_bf16(wdt, 3))
    eout_ref[...] = sum(jnp.dot(p, expand, preferred_element_type=f32) for p in _split_bf16(e_out, 3))

    lane = lax.broadcasted_iota(jnp.int32, (CHUNK, LANES), 1)
    low_half = lane < SSM_HEAD_DIM
    for g in range(SSM_GROUPS):
        gcols = slice(g * GROUP_COLS, (g + 1) * GROUP_COLS)
        ncols = slice(g * SSM_STATE, (g + 1) * SSM_STATE)
        bg = bm_ref[:, ncols]
        cg = cm_ref[:, ncols]
        cb = lax.dot_general(cg, bg, (((1,), (1,)), ((), ())), preferred_element_type=f32)
        y_pairs = []
        for pr in range(2):
            scores = []
            for hh in range(2):
                h = 4 * g + 2 * pr + hh
                diff = a_cs[:, h:h + 1] - a_cs_t[h:h + 1, :]
                decay = jnp.exp(jnp.where(causal, diff, -jnp.inf))
                scores.append((cb * decay * dt_t[h:h + 1, :]).astype(bf16))
            lhs = jnp.concatenate(scores, axis=1)
            xp = xs_ref[:, g * GROUP_COLS + pr * LANES:g * GROUP_COLS + (pr + 1) * LANES]
            rhs = jnp.concatenate([jnp.where(low_half, xp, 0.0), jnp.where(low_half, 0.0, xp)], axis=0)
            y_pairs.append(jnp.dot(lhs, rhs.astype(bf16), preferred_element_type=f32))
        y = jnp.concatenate(y_pairs, axis=1)
        xs_g = xs_ref[:, gcols]
        xdw = (xs_g * wdt_ref[:, gcols]).astype(bf16)
        st_new = lax.dot_general(bg, xdw, (((0,), (0,)), ((), ())), preferred_element_type=f32)
        prev = state_ref[g]
        y = y + jnp.dot(cg, prev.astype(bf16), preferred_element_type=f32) * eout_ref[:, gcols]
        state_ref[g] = prev * eout_ref[CHUNK - 1:CHUNK, gcols] + st_new
        y = y + dskip_ref[:, gcols] * xs_g
        y = y * jax.nn.silu(z_ref[:, gcols])
        y = y * lax.rsqrt(jnp.mean(y * y, axis=-1, keepdims=True) + RMS_EPS)
        o_ref[:, gcols] = (y * ng_ref[:, gcols]).astype(o_ref.dtype)


def _ssd(proj, dtr, conv_w, conv_b, dt_bias, a_log, d_skip, norm_g, bsz):
    t = proj.shape[0]
    nc = SEQ // CHUNK
    halo_blocks = CHUNK // CONV_HALO
    row = lambda b, c: b * nc + c
    const = lambda b, c: (0, 0)
    return pl.pallas_call(
        _ssd_kernel,
        grid=(bsz, nc),
        in_specs=[
            pl.BlockSpec((CHUNK, SSM_CONV_DIM), lambda b, c: (row(b, c), XBC_OFF // SSM_CONV_DIM)),
            pl.BlockSpec((CONV_HALO, SSM_CONV_DIM),
                         lambda b, c: (jnp.maximum(row(b, c) * halo_blocks - 1, 0), XBC_OFF // SSM_CONV_DIM)),
            pl.BlockSpec((CHUNK, SSM_INNER), lambda b, c: (row(b, c), Z_OFF // SSM_INNER)),
            pl.BlockSpec((CHUNK, DT_PAD), lambda b, c: (row(b, c), 0)),
            pl.BlockSpec((SSM_CONV, SSM_CONV_DIM), const),
            pl.BlockSpec((1, SSM_CONV_DIM), const),
            pl.BlockSpec((1, DT_PAD), const),
            pl.BlockSpec((1, DT_PAD), const),
            pl.BlockSpec((1, SSM_INNER), const),
            pl.BlockSpec((1, SSM_INNER), const),
        ],
        out_specs=pl.BlockSpec((CHUNK, SSM_INNER), lambda b, c: (row(b, c), 0)),
        out_shape=jax.ShapeDtypeStruct((t, SSM_INNER), bf16),
        scratch_shapes=[
            pltpu.VMEM((SSM_GROUPS, SSM_STATE, GROUP_COLS), f32),
            pltpu.VMEM((CHUNK, SSM_INNER), f32),
            pltpu.VMEM((CHUNK, SSM_GROUPS * SSM_STATE), bf16),
            pltpu.VMEM((CHUNK, SSM_GROUPS * SSM_STATE), bf16),
            pltpu.VMEM((CHUNK, SSM_INNER), f32),
            pltpu.VMEM((CHUNK, SSM_INNER), f32),
        ],
        compiler_params=_params("parallel", "arbitrary"),
        name="ssd",
    )(proj, proj, proj, dtr, conv_w, conv_b, dt_bias, a_log, d_skip, norm_g)


def _merge_kernel(a_ref, b_ref, c_ref, wa_ref, wb_ref, wc_ref, ga_ref, gb_ref, gc_ref, o_ref):
    ya = jnp.dot(a_ref[...], wa_ref[...], preferred_element_type=f32)
    yb = jnp.dot(b_ref[...], wb_ref[...], preferred_element_type=f32)
    yc = jnp.dot(c_ref[...], wc_ref[...], preferred_element_type=f32)
    merged = (jax.nn.sigmoid(ga_ref[...]) * ya + jax.nn.sigmoid(gb_ref[...]) * yb
              + jax.nn.sigmoid(gc_ref[...]) * yc)
    o_ref[...] = merged.astype(o_ref.dtype)


def _merge(ma, gb_, yn, proj, wa, wb, wc, tm=1024, tn=512):
    t = proj.shape[0]
    d = D_MODEL
    gate_blk = GATE_OFF // tn
    per_gate = d // tn
    return pl.pallas_call(
        _merge_kernel,
        grid=(t // tm, d // tn),
        in_specs=[
            pl.BlockSpec((tm, POOL_WIDTH), lambda i, j: (i, 0)),
            pl.BlockSpec((tm, GMLP_WIDTH), lambda i, j: (i, 0)),
            pl.BlockSpec((tm, SSM_INNER), lambda i, j: (i, 0)),
            pl.BlockSpec((POOL_WIDTH, tn), lambda i, j: (0, j)),
            pl.BlockSpec((GMLP_WIDTH, tn), lambda i, j: (0, j)),
            pl.BlockSpec((SSM_INNER, tn), lambda i, j: (0, j)),
            pl.BlockSpec((tm, tn), lambda i, j: (i, gate_blk + j)),
            pl.BlockSpec((tm, tn), lambda i, j: (i, gate_blk + per_gate + j)),
            pl.BlockSpec((tm, tn), lambda i, j: (i, gate_blk + 2 * per_gate + j)),
        ],
        out_specs=pl.BlockSpec((tm, tn), lambda i, j: (i, j)),
        out_shape=jax.ShapeDtypeStruct((t, d), bf16),
        compiler_params=_params("parallel", "arbitrary"),
        name="merge",
    )(ma, gb_, yn, wa, wb, wc, proj, proj, proj)


def _wo_kernel(m_ref, w_ref, x_ref, g_ref, o_ref):
    o_ref[...] = x_ref[...] + g_ref[0] * jnp.dot(m_ref[...], w_ref[...], preferred_element_type=f32)


def _wo(merged, w_o, xf, mod, tm=1024, tn=512):
    t, d = xf.shape
    tiles_per_batch = SEQ // tm
    return pl.pallas_call(
        _wo_kernel,
        grid=(t // tm, d // tn),
        in_specs=[
            pl.BlockSpec((tm, d), lambda i, j: (i, 0)),
            pl.BlockSpec((d, tn), lambda i, j: (0, j)),
            pl.BlockSpec((tm, tn), lambda i, j: (i, j)),
            pl.BlockSpec((1, 1, tn), lambda i, j: ((i // tiles_per_batch) * N_ADA + G1, 0, j)),
        ],
        out_specs=pl.BlockSpec((tm, tn), lambda i, j: (i, j)),
        out_shape=jax.ShapeDtypeStruct((t, d), f32),
        compiler_params=_params("parallel", "arbitrary"),
        name="wo",
    )(merged, w_o, xf, mod)


def _mlp_kernel(x_ref, sh_ref, sc_ref, g_ref, wu_ref, wd_ref, fn_ref, o_ref, h_ref, *, final_norm):
    f = pl.program_id(1)

    @pl.when(f == 0)
    def _():
        _mod_rmsnorm(x_ref, sh_ref, sc_ref, h_ref)
        o_ref[...] = jnp.zeros_like(o_ref)

    u = jnp.dot(h_ref[...], wu_ref[0], preferred_element_type=f32)
    u = jnp.square(jnp.maximum(u, 0.0)).astype(bf16)
    o_ref[...] += jnp.dot(u, wd_ref[...], preferred_element_type=f32)

    @pl.when(f == pl.num_programs(1) - 1)
    def _():
        rows = 256
        gate = g_ref[0]

        def body(r, carry):
            rs = pl.ds(pl.multiple_of(r * rows, rows), rows)
            xn = x_ref[rs, :] + gate * o_ref[rs, :]
            if final_norm:
                ms = jnp.mean(xn * xn, axis=-1, keepdims=True)
                xn = (xn * lax.rsqrt(ms + RMS_EPS)) * fn_ref[...]
            o_ref[rs, :] = xn
            return carry

        lax.fori_loop(0, x_ref.shape[0] // rows, body, 0)


def _mlp(xf, mod, w_up, w_down, fn, final_norm, tm=1024):
    t, d = xf.shape
    tf = w_up.shape[2]
    tiles_per_batch = SEQ // tm
    mod_spec = lambda k: pl.BlockSpec((1, 1, d), lambda i, f: ((i // tiles_per_batch) * N_ADA + k, 0, 0))
    return pl.pallas_call(
        functools.partial(_mlp_kernel, final_norm=final_norm),
        grid=(t // tm, D_FF // tf),
        in_specs=[
            pl.BlockSpec((tm, d), lambda i, f: (i, 0)),
            mod_spec(SH2),
            mod_spec(SC2),
            mod_spec(G2),
            pl.BlockSpec((1, d, tf), lambda i, f: (f, 0, 0)),
            pl.BlockSpec((tf, d), lambda i, f: (f, 0)),
            pl.BlockSpec((1, d), lambda i, f: (0, 0)),
        ],
        out_specs=pl.BlockSpec((tm, d), lambda i, f: (i, 0)),
        out_shape=jax.ShapeDtypeStruct((t, d), f32),
        scratch_shapes=[pltpu.VMEM((tm, d), bf16)],
        compiler_params=_params("parallel", "arbitrary"),
        name="mlp",
    )(xf, mod, mod, mod, w_up, w_down, fn)


def _pad_lanes(v, fill=0.0):
    return jnp.pad(v.astype(f32), (0, DT_PAD - v.shape[0]), constant_values=fill).reshape(1, DT_PAD)


def kernel(x, c, w_ada, b_ada, w_in, pool_w, pool_scale, gmlp_ln_g, gmlp_ln_b, gmlp_ws, gmlp_bs, conv_w, conv_b, dt_bias, a_log, d_skip, ssm_norm, w_pool_out, w_gmlp_out, w_ssm_out, w_o, w_up, w_down, final_norm):
    bsz, seq, d = x.shape
    assert (seq, d) == (SEQ, D_MODEL) and w_ada.shape[0] == DEPTH
    xf = x.reshape(bsz * seq, d)
    ada = _ada(c, w_ada, b_ada)
    fn = final_norm.reshape(1, d)
    o_uv, o_z, o_xbc, o_dt, o_gate = 1024, 3072, 5120, 9216, 9248
    for l in range(DEPTH):
        mod = ada[l].reshape(bsz * N_ADA, 1, d)
        wl = w_in[l]
        w_main = jnp.concatenate(
            [wl[:, o_xbc:o_dt], wl[:, o_gate:], wl[:, o_uv:o_z], wl[:, o_z:o_xbc], wl[:, :o_uv]], axis=1
        ).astype(bf16)
        w_dt = jnp.pad(wl[:, o_dt:o_gate], ((0, 0), (0, DT_PAD - SSM_HEADS))).astype(bf16)
        proj, dtr = _inproj(xf, mod, _col_blocked(w_main, 512 if l == 0 else 1024), w_dt)
        mixed_a = _pool(proj, pool_w[l].astype(bf16), pool_scale[l].reshape(1, POOL_WIDTH))
        gated_b = _gmlp(proj, gmlp_ln_g[l].reshape(1, GMLP_WIDTH), gmlp_ln_b[l].reshape(1, GMLP_WIDTH),
                        gmlp_ws[l], gmlp_bs[l].reshape(GMLP_HEADS, CHUNK, 1))
        yn = _ssd(proj, dtr, conv_w[l], conv_b[l].reshape(1, SSM_CONV_DIM), _pad_lanes(dt_bias[l]),
                  _pad_lanes(a_log[l]), jnp.repeat(d_skip[l], SSM_HEAD_DIM).reshape(1, SSM_INNER),
                  ssm_norm[l].reshape(1, SSM_INNER), bsz)
        merged = _merge(mixed_a, gated_b, yn, proj, w_pool_out[l].astype(bf16), w_gmlp_out[l].astype(bf16),
                        w_ssm_out[l].astype(bf16))
        xf = _wo(merged, w_o[l].astype(bf16), xf, mod)
        w_up_b = _col_blocked(w_up[l].astype(bf16), 512 if l == 0 else 1024)
        xf = _mlp(xf, mod, w_up_b, w_down[l].astype(bf16), fn, final_norm=(l == DEPTH - 1),
                  tm=1024 if l == 0 else 512)
    return xf.reshape(bsz, seq, d)
```

```python
import functools

import jax
import jax.numpy as jnp
from jax import lax
from jax.experimental import pallas as pl
from jax.experimental.pallas import tpu as pltpu

f32 = jnp.float32
bf16 = jnp.bfloat16

D_MODEL = 2048
SEQ = 2048
DEPTH = 2
POOL_WIDTH = 1024
POOL_WINDOWS = (2, 4, 8, 16)
POOL_GROUP_DIM = 256
POOL_HALO = 16
GMLP_WIDTH = 1024
GMLP_HEADS = 8
GMLP_HEAD_DIM = 128
CHUNK = 128
SSM_INNER = 2048
SSM_HEAD_DIM = 64
SSM_HEADS = 32
SSM_GROUPS = 8
SSM_STATE = 128
SSM_CONV = 4
CONV_HALO = 8
SSM_CONV_DIM = 4096
GROUP_COLS = SSM_INNER // SSM_GROUPS
N_ADA = 6
D_FF = 8192
RMS_EPS = 1e-6
LN_EPS = 1e-5
LANES = 128
DT_PAD = LANES

XBC_OFF = 0
GATE_OFF = 4096
UV_OFF = 10240
Z_OFF = 12288
P_OFF = 14336
N_MAIN = 15360

VMEM_LIMIT = 56 * 1024 * 1024

SH1, SC1, G1, SH2, SC2, G2 = range(N_ADA)


def _params(*sem):
    return pltpu.CompilerParams(dimension_semantics=sem, vmem_limit_bytes=VMEM_LIMIT)


def _split_bf16(x, n):
    parts, r = [], x
    for _ in range(n):
        p = r.astype(bf16)
        parts.append(p)
        r = r - p.astype(f32)
    return parts


def _mod_rmsnorm(x_ref, sh_ref, sc_ref, h_ref, rows=256):
    sh = sh_ref[0]
    sc1p = 1.0 + sc_ref[0]

    def body(r, carry):
        rs = pl.ds(pl.multiple_of(r * rows, rows), rows)
        xf = x_ref[rs, :]
        ms = jnp.mean(xf * xf, axis=-1, keepdims=True)
        h_ref[rs, :] = ((xf * lax.rsqrt(ms + RMS_EPS)) * sc1p + sh).astype(h_ref.dtype)
        return carry

    lax.fori_loop(0, x_ref.shape[0] // rows, body, 0)


def _ada_kernel(c_ref, w_ref, b_ref, o_ref):
    c_act = jax.nn.silu(c_ref[...]).astype(bf16)
    part = jnp.dot(c_act, w_ref[0].astype(bf16), preferred_element_type=f32)

    @pl.when(pl.program_id(1) == 0)
    def _():
        o_ref[0] = part + b_ref[0]

    @pl.when(pl.program_id(1) > 0)
    def _():
        o_ref[0] += part


def _ada(c, w_ada, b_ada, tk=256):
    depth, d, n = w_ada.shape
    bsz = c.shape[0]
    return pl.pallas_call(
        _ada_kernel,
        grid=(depth, d // tk),
        in_specs=[
            pl.BlockSpec((bsz, tk), lambda l, k: (0, k)),
            pl.BlockSpec((1, tk, n), lambda l, k: (l, k, 0)),
            pl.BlockSpec((1, 1, n), lambda l, k: (l, 0, 0)),
        ],
        out_specs=pl.BlockSpec((1, bsz, n), lambda l, k: (l, 0, 0)),
        out_shape=jax.ShapeDtypeStruct((depth, bsz, n), f32),
        compiler_params=_params("parallel", "arbitrary"),
        name="ada",
    )(c, w_ada, b_ada.reshape(depth, 1, n))


def _inproj_kernel(x_ref, sh_ref, sc_ref, w_ref, wdt_ref, o_ref, odt_ref, h_ref):
    @pl.when(pl.program_id(1) == 0)
    def _():
        _mod_rmsnorm(x_ref, sh_ref, sc_ref, h_ref)
        odt_ref[...] = jnp.dot(h_ref[...], wdt_ref[...], preferred_element_type=f32)

    o_ref[...] = jnp.dot(h_ref[...], w_ref[...], preferred_element_type=f32)


def _inproj(xf, mod, w_main, w_dt, tm=1024, tn=1024):
    t, d = xf.shape
    tiles_per_seq = SEQ // tm
    return pl.pallas_call(
        _inproj_kernel,
        grid=(t // tm, N_MAIN // tn),
        in_specs=[
            pl.BlockSpec((tm, d), lambda i, j: (i, 0)),
            pl.BlockSpec((1, 1, d), lambda i, j: ((i // tiles_per_seq) * N_ADA + SH1, 0, 0)),
            pl.BlockSpec((1, 1, d), lambda i, j: ((i // tiles_per_seq) * N_ADA + SC1, 0, 0)),
            pl.BlockSpec((d, tn), lambda i, j: (0, j)),
            pl.BlockSpec((d, DT_PAD), lambda i, j: (0, 0)),
        ],
        out_specs=[
            pl.BlockSpec((tm, tn), lambda i, j: (i, j)),
            pl.BlockSpec((tm, DT_PAD), lambda i, j: (i, 0)),
        ],
        out_shape=[
            jax.ShapeDtypeStruct((t, N_MAIN), f32),
            jax.ShapeDtypeStruct((t, DT_PAD), f32),
        ],
        scratch_shapes=[pltpu.VMEM((tm, d), bf16)],
        compiler_params=_params("parallel", "arbitrary"),
        name="inproj",
    )(xf, mod, mod, w_main, w_dt)


def _pool_kernel(p_ref, halo_ref, pw_ref, scale_ref, o_ref, *, tiles_per_seq):
    ts = p_ref.shape[0]
    tile_in_seq = pl.program_id(0) % tiles_per_seq
    first = tile_in_seq == 0
    pos = lax.broadcasted_iota(jnp.int32, (ts, 1), 0) + tile_in_seq * ts
    for g, w in enumerate(POOL_WINDOWS):
        cols = slice(g * POOL_GROUP_DIM, (g + 1) * POOL_GROUP_DIM)
        pg = p_ref[:, cols]
        halo = jnp.where(first, 0.0, halo_ref[:, cols])
        s = jnp.concatenate([halo, pg], axis=0)
        k = 1
        while k < w:
            s = s + pltpu.roll(s, k, axis=0)
            k *= 2
        count = jnp.minimum(pos + 1, w).astype(f32)
        pooled = s[POOL_HALO:, :] / count - pg
        mixed = jnp.dot(pooled.astype(bf16), pw_ref[g], preferred_element_type=f32)
        o_ref[:, cols] = (mixed * scale_ref[:, cols]).astype(o_ref.dtype)


def _pool(proj, pool_w, pool_scale, ts=512):
    t = proj.shape[0]
    halo_blocks = ts // POOL_HALO
    p_blk = P_OFF // POOL_WIDTH
    return pl.pallas_call(
        functools.partial(_pool_kernel, tiles_per_seq=SEQ // ts),
        grid=(t // ts,),
        in_specs=[
            pl.BlockSpec((ts, POOL_WIDTH), lambda i: (i, p_blk)),
            pl.BlockSpec((POOL_HALO, POOL_WIDTH), lambda i: (jnp.maximum(i * halo_blocks - 1, 0), p_blk)),
            pl.BlockSpec(pool_w.shape, lambda i: (0, 0, 0)),
            pl.BlockSpec((1, POOL_WIDTH), lambda i: (0, 0)),
        ],
        out_specs=pl.BlockSpec((ts, POOL_WIDTH), lambda i: (i, 0)),
        out_shape=jax.ShapeDtypeStruct((t, POOL_WIDTH), bf16),
        compiler_params=_params("parallel"),
        name="pool",
    )(proj, proj, pool_w, pool_scale)


def _gelu(x):
    return 0.5 * x * (1.0 + lax.erf(x * 0.7071067811865476))


def _gmlp_kernel(u_ref, v_ref, g_ref, b_ref, ws_ref, bs_ref, o_ref, vn_ref):
    tg = u_ref.shape[0]
    v = _gelu(v_ref[...])
    mu = jnp.mean(v, axis=-1, keepdims=True)
    dv = v - mu
    var = jnp.mean(dv * dv, axis=-1, keepdims=True)
    vn_ref[...] = ((dv * lax.rsqrt(var + LN_EPS)) * g_ref[...] + b_ref[...]).astype(vn_ref.dtype)
    ri = lax.broadcasted_iota(jnp.int32, (CHUNK, CHUNK), 0)
    ci = lax.broadcasted_iota(jnp.int32, (CHUNK, CHUNK), 1)
    causal = ri >= ci
    for h in range(GMLP_HEADS):
        cols = slice(h * GMLP_HEAD_DIM, (h + 1) * GMLP_HEAD_DIM)
        wm = jnp.where(causal, ws_ref[h], 0.0).astype(bf16)
        bias = bs_ref[h]
        for c in range(tg // CHUNK):
            rows = slice(c * CHUNK, (c + 1) * CHUNK)
            mixed = jnp.dot(wm, vn_ref[rows, cols], preferred_element_type=f32) + bias
            o_ref[rows, cols] = (_gelu(u_ref[rows, cols]) * mixed).astype(o_ref.dtype)


def _gmlp(proj, ln_g, ln_b, ws, bs, tg=512):
    t = proj.shape[0]
    u_blk = UV_OFF // GMLP_WIDTH
    return pl.pallas_call(
        _gmlp_kernel,
        grid=(t // tg,),
        in_specs=[
            pl.BlockSpec((tg, GMLP_WIDTH), lambda i: (i, u_blk)),
            pl.BlockSpec((tg, GMLP_WIDTH), lambda i: (i, u_blk + 1)),
            pl.BlockSpec((1, GMLP_WIDTH), lambda i: (0, 0)),
            pl.BlockSpec((1, GMLP_WIDTH), lambda i: (0, 0)),
            pl.BlockSpec((GMLP_HEADS, CHUNK, CHUNK), lambda i: (0, 0, 0)),
            pl.BlockSpec((GMLP_HEADS, CHUNK, 1), lambda i: (0, 0, 0)),
        ],
        out_specs=pl.BlockSpec((tg, GMLP_WIDTH), lambda i: (i, 0)),
        out_shape=jax.ShapeDtypeStruct((t, GMLP_WIDTH), bf16),
        scratch_shapes=[pltpu.VMEM((tg, GMLP_WIDTH), bf16)],
        compiler_params=_params("parallel"),
        name="gmlp",
    )(proj, proj, ln_g, ln_b, ws, bs)


def _ssd_chunk(rows, dtr_ref, dtb_ref, alog_ref, z_ref, dskip_ref, ng_ref, o_ref, state_ref, xs_ref,
               bm_ref, cm_ref, wdt_ref, eout_ref, consts):
    causal, tril, expand, low_half = consts
    dt = jax.nn.softplus(dtr_ref[rows, :] + dtb_ref[...])
    da = dt * (-jnp.exp(alog_ref[...]))
    cs3 = jnp.dot(tril, jnp.concatenate(_split_bf16(da, 3), axis=1), preferred_element_type=f32)
    a_cs = cs3[:, :LANES] + cs3[:, LANES:2 * LANES] + cs3[:, 2 * LANES:]
    a_last = a_cs[CHUNK - 1:CHUNK, :]
    wdt = dt * jnp.exp(a_last - a_cs)
    e_out = jnp.exp(a_cs)
    a_cs_t = a_cs.T
    dt_t = dt.T
    pieces = jnp.concatenate(_split_bf16(wdt, 3) + _split_bf16(e_out, 3), axis=0)
    ex = jnp.dot(pieces, expand, preferred_element_type=f32)
    wdt_ref[rows, :] = ex[:CHUNK] + ex[CHUNK:2 * CHUNK] + ex[2 * CHUNK:3 * CHUNK]
    eout_ref[rows, :] = ex[3 * CHUNK:4 * CHUNK] + ex[4 * CHUNK:5 * CHUNK] + ex[5 * CHUNK:]
    last = rows.start + CHUNK - 1

    for g in range(SSM_GROUPS):
        gcols = slice(g * GROUP_COLS, (g + 1) * GROUP_COLS)
        ncols = slice(g * SSM_STATE, (g + 1) * SSM_STATE)
        bg = bm_ref[rows, ncols]
        cg = cm_ref[rows, ncols]
        cb = lax.dot_general(cg, bg, (((1,), (1,)), ((), ())), preferred_element_type=f32)
        y_pairs = []
        for pr in range(2):
            scores = []
            for hh in range(2):
                h = 4 * g + 2 * pr + hh
                diff = a_cs[:, h:h + 1] - a_cs_t[h:h + 1, :]
                decay = jnp.exp(jnp.where(causal, diff, -jnp.inf))
                scores.append((cb * decay * dt_t[h:h + 1, :]).astype(bf16))
            lhs = jnp.concatenate(scores, axis=1)
            xp = xs_ref[rows, g * GROUP_COLS + pr * LANES:g * GROUP_COLS + (pr + 1) * LANES]
            rhs = jnp.concatenate([jnp.where(low_half, xp, 0.0), jnp.where(low_half, 0.0, xp)], axis=0)
            y_pairs.append(jnp.dot(lhs, rhs.astype(bf16), preferred_element_type=f32))
        y = jnp.concatenate(y_pairs, axis=1)
        xs_g = xs_ref[rows, gcols]
        xdw = (xs_g * wdt_ref[rows, gcols]).astype(bf16)
        st_new = lax.dot_general(bg, xdw, (((0,), (0,)), ((), ())), preferred_element_type=f32)
        prev = state_ref[g]
        y = y + jnp.dot(cg, prev.astype(bf16), preferred_element_type=f32) * eout_ref[rows, gcols]
        state_ref[g] = prev * eout_ref[last:last + 1, gcols] + st_new
        y = y + dskip_ref[:, gcols] * xs_g
        y = y * jax.nn.silu(z_ref[rows, gcols])
        y = y * lax.rsqrt(jnp.mean(y * y, axis=-1, keepdims=True) + RMS_EPS)
        o_ref[rows, gcols] = (y * ng_ref[:, gcols]).astype(o_ref.dtype)


def _ssd_kernel(xbc_ref, halo_ref, z_ref, dtr_ref, cw_ref, cb_ref, dtb_ref, alog_ref, dskip_ref,
                ng_ref, o_ref, state_ref, xs_ref, bm_ref, cm_ref, wdt_ref, eout_ref):
    step = pl.program_id(1)

    @pl.when(step == 0)
    def _():
        state_ref[...] = jnp.zeros_like(state_ref)

    slab = 512
    for s in range(SSM_CONV_DIM // slab):
        cols = slice(s * slab, (s + 1) * slab)
        halo = jnp.where(step == 0, 0.0, halo_ref[:, cols])
        e = jnp.concatenate([halo, xbc_ref[:, cols]], axis=0)
        w = cw_ref[:, cols]
        acc = e[CONV_HALO:, :] * w[SSM_CONV - 1:SSM_CONV, :]
        for k in range(1, SSM_CONV):
            acc = acc + pltpu.roll(e, k, axis=0)[CONV_HALO:, :] * w[SSM_CONV - 1 - k:SSM_CONV - k, :]
        act = jax.nn.silu(acc + cb_ref[:, cols])
        if s * slab < SSM_INNER:
            xs_ref[:, cols] = act
        elif s * slab < SSM_INNER + SSM_GROUPS * SSM_STATE:
            off = s * slab - SSM_INNER
            bm_ref[:, off:off + slab] = act.astype(bf16)
        else:
            off = s * slab - SSM_INNER - SSM_GROUPS * SSM_STATE
            cm_ref[:, off:off + slab] = act.astype(bf16)

    ri = lax.broadcasted_iota(jnp.int32, (CHUNK, CHUNK), 0)
    ci = lax.broadcasted_iota(jnp.int32, (CHUNK, CHUNK), 1)
    causal = ri >= ci
    tril = jnp.where(causal, 1.0, 0.0).astype(bf16)
    hi = lax.broadcasted_iota(jnp.int32, (LANES, SSM_INNER), 0)
    ch = lax.broadcasted_iota(jnp.int32, (LANES, SSM_INNER), 1)
    expand = jnp.where(hi == ch // SSM_HEAD_DIM, 1.0, 0.0).astype(bf16)
    low_half = lax.broadcasted_iota(jnp.int32, (CHUNK, LANES), 1) < SSM_HEAD_DIM
    consts = (causal, tril, expand, low_half)
    for cc in range(xbc_ref.shape[0] // CHUNK):
        _ssd_chunk(slice(cc * CHUNK, (cc + 1) * CHUNK), dtr_ref, dtb_ref, alog_ref, z_ref, dskip_ref,
                   ng_ref, o_ref, state_ref, xs_ref, bm_ref, cm_ref, wdt_ref, eout_ref, consts)


def _ssd(proj, dtr, conv_w, conv_b, dt_bias, a_log, d_skip, norm_g, bsz, chunks_per_step=2):
    t = proj.shape[0]
    rb = chunks_per_step * CHUNK
    steps = SEQ // rb
    halo_blocks = rb // CONV_HALO
    row = lambda b, c: b * steps + c
    const = lambda b, c: (0, 0)
    return pl.pallas_call(
        _ssd_kernel,
        grid=(bsz, steps),
        in_specs=[
            pl.BlockSpec((rb, SSM_CONV_DIM), lambda b, c: (row(b, c), XBC_OFF // SSM_CONV_DIM)),
            pl.BlockSpec((CONV_HALO, SSM_CONV_DIM),
                         lambda b, c: (jnp.maximum(row(b, c) * halo_blocks - 1, 0), XBC_OFF // SSM_CONV_DIM)),
            pl.BlockSpec((rb, SSM_INNER), lambda b, c: (row(b, c), Z_OFF // SSM_INNER)),
            pl.BlockSpec((rb, DT_PAD), lambda b, c: (row(b, c), 0)),
            pl.BlockSpec((SSM_CONV, SSM_CONV_DIM), const),
            pl.BlockSpec((1, SSM_CONV_DIM), const),
            pl.BlockSpec((1, DT_PAD), const),
            pl.BlockSpec((1, DT_PAD), const),
            pl.BlockSpec((1, SSM_INNER), const),
            pl.BlockSpec((1, SSM_INNER), const),
        ],
        out_specs=pl.BlockSpec((rb, SSM_INNER), lambda b, c: (row(b, c), 0)),
        out_shape=jax.ShapeDtypeStruct((t, SSM_INNER), bf16),
        scratch_shapes=[
            pltpu.VMEM((SSM_GROUPS, SSM_STATE, GROUP_COLS), f32),
            pltpu.VMEM((rb, SSM_INNER), f32),
            pltpu.VMEM((rb, SSM_GROUPS * SSM_STATE), bf16),
            pltpu.VMEM((rb, SSM_GROUPS * SSM_STATE), bf16),
            pltpu.VMEM((rb, SSM_INNER), f32),
            pltpu.VMEM((rb, SSM_INNER), f32),
        ],
        compiler_params=_params("parallel", "arbitrary"),
        name="ssd",
    )(proj, proj, proj, dtr, conv_w, conv_b, dt_bias, a_log, d_skip, norm_g)


def _merge_kernel(a_ref, b_ref, c_ref, wa_ref, wb_ref, wc_ref, ga_ref, gb_ref, gc_ref, o_ref):
    ya = jnp.dot(a_ref[...], wa_ref[...], preferred_element_type=f32)
    yb = jnp.dot(b_ref[...], wb_ref[...], preferred_element_type=f32)
    yc = jnp.dot(c_ref[...], wc_ref[...], preferred_element_type=f32)
    merged = (jax.nn.sigmoid(ga_ref[...]) * ya + jax.nn.sigmoid(gb_ref[...]) * yb
              + jax.nn.sigmoid(gc_ref[...]) * yc)
    o_ref[...] = merged.astype(o_ref.dtype)


def _merge(ma, gb_, yn, proj, wa, wb, wc, tm=1024, tn=512):
    t = proj.shape[0]
    d = D_MODEL
    gate_blk = GATE_OFF // tn
    per_gate = d // tn
    return pl.pallas_call(
        _merge_kernel,
        grid=(t // tm, d // tn),
        in_specs=[
            pl.BlockSpec((tm, POOL_WIDTH), lambda i, j: (i, 0)),
            pl.BlockSpec((tm, GMLP_WIDTH), lambda i, j: (i, 0)),
            pl.BlockSpec((tm, SSM_INNER), lambda i, j: (i, 0)),
            pl.BlockSpec((POOL_WIDTH, tn), lambda i, j: (0, j)),
            pl.BlockSpec((GMLP_WIDTH, tn), lambda i, j: (0, j)),
            pl.BlockSpec((SSM_INNER, tn), lambda i, j: (0, j)),
            pl.BlockSpec((tm, tn), lambda i, j: (i, gate_blk + j)),
            pl.BlockSpec((tm, tn), lambda i, j: (i, gate_blk + per_gate + j)),
            pl.BlockSpec((tm, tn), lambda i, j: (i, gate_blk + 2 * per_gate + j)),
        ],
        out_specs=pl.BlockSpec((tm, tn), lambda i, j: (i, j)),
        out_shape=jax.ShapeDtypeStruct((t, d), bf16),
        compiler_params=_params("parallel", "arbitrary"),
        name="merge",
    )(ma, gb_, yn, wa, wb, wc, proj, proj, proj)


def _wo_kernel(m_ref, w_ref, x_ref, g_ref, o_ref):
    o_ref[...] = x_ref[...] + g_ref[0] * jnp.dot(m_ref[...], w_ref[...], preferred_element_type=f32)


def _wo(merged, w_o, xf, mod, tm=1024, tn=512):
    t, d = xf.shape
    tiles_per_batch = SEQ // tm
    return pl.pallas_call(
        _wo_kernel,
        grid=(t // tm, d // tn),
        in_specs=[
            pl.BlockSpec((tm, d), lambda i, j: (i, 0)),
            pl.BlockSpec((d, tn), lambda i, j: (0, j)),
            pl.BlockSpec((tm, tn), lambda i, j: (i, j)),
            pl.BlockSpec((1, 1, tn), lambda i, j: ((i // tiles_per_batch) * N_ADA + G1, 0, j)),
        ],
        out_specs=pl.BlockSpec((tm, tn), lambda i, j: (i, j)),
        out_shape=jax.ShapeDtypeStruct((t, d), f32),
        compiler_params=_params("parallel", "arbitrary"),
        name="wo",
    )(merged, w_o, xf, mod)


def _mlp_kernel(x_ref, sh_ref, sc_ref, g_ref, wu_ref, wd_ref, fn_ref, o_ref, h_ref, *, final_norm):
    f = pl.program_id(1)

    @pl.when(f == 0)
    def _():
        _mod_rmsnorm(x_ref, sh_ref, sc_ref, h_ref)
        o_ref[...] = jnp.zeros_like(o_ref)

    u = jnp.dot(h_ref[...], wu_ref[...], preferred_element_type=f32)
    u = jnp.square(jnp.maximum(u, 0.0)).astype(bf16)
    o_ref[...] += jnp.dot(u, wd_ref[...], preferred_element_type=f32)

    @pl.when(f == pl.num_programs(1) - 1)
    def _():
        rows = 256
        gate = g_ref[0]

        def body(r, carry):
            rs = pl.ds(pl.multiple_of(r * rows, rows), rows)
            xn = x_ref[rs, :] + gate * o_ref[rs, :]
            if final_norm:
                ms = jnp.mean(xn * xn, axis=-1, keepdims=True)
                xn = (xn * lax.rsqrt(ms + RMS_EPS)) * fn_ref[...]
            o_ref[rs, :] = xn
            return carry

        lax.fori_loop(0, x_ref.shape[0] // rows, body, 0)


def _mlp(xf, mod, w_up, w_down, fn, final_norm, tm=1024, tf=512):
    t, d = xf.shape
    tiles_per_batch = SEQ // tm
    mod_spec = lambda k: pl.BlockSpec((1, 1, d), lambda i, f: ((i // tiles_per_batch) * N_ADA + k, 0, 0))
    return pl.pallas_call(
        functools.partial(_mlp_kernel, final_norm=final_norm),
        grid=(t // tm, D_FF // tf),
        in_specs=[
            pl.BlockSpec((tm, d), lambda i, f: (i, 0)),
            mod_spec(SH2),
            mod_spec(SC2),
            mod_spec(G2),
            pl.BlockSpec((d, tf), lambda i, f: (0, f)),
            pl.BlockSpec((tf, d), lambda i, f: (f, 0)),
            pl.BlockSpec((1, d), lambda i, f: (0, 0)),
        ],
        out_specs=pl.BlockSpec((tm, d), lambda i, f: (i, 0)),
        out_shape=jax.ShapeDtypeStruct((t, d), f32),
        scratch_shapes=[pltpu.VMEM((tm, d), bf16)],
        compiler_params=_params("parallel", "arbitrary"),
        name="mlp",
    )(xf, mod, mod, mod, w_up, w_down, fn)


def _cast_kernel(w_ref, o_ref):
    o_ref[...] = w_ref[...].astype(o_ref.dtype)


def _to_bf16(w, block_bytes=8 * 1024 * 1024):
    r, c = w.shape
    rows = min(r, block_bytes // (4 * c))
    return pl.pallas_call(
        _cast_kernel,
        grid=(r // rows,),
        in_specs=[pl.BlockSpec((rows, c), lambda i: (i, 0))],
        out_specs=pl.BlockSpec((rows, c), lambda i: (i, 0)),
        out_shape=jax.ShapeDtypeStruct((r, c), bf16),
        compiler_params=_params("parallel"),
        name="cast",
    )(w)


def _pad_lanes(v, fill=0.0):
    return jnp.pad(v.astype(f32), (0, DT_PAD - v.shape[0]), constant_values=fill).reshape(1, DT_PAD)


def kernel(x, c, w_ada, b_ada, w_in, pool_w, pool_scale, gmlp_ln_g, gmlp_ln_b, gmlp_ws, gmlp_bs, conv_w, conv_b, dt_bias, a_log, d_skip, ssm_norm, w_pool_out, w_gmlp_out, w_ssm_out, w_o, w_up, w_down, final_norm):
    bsz, seq, d = x.shape
    assert (seq, d) == (SEQ, D_MODEL) and w_ada.shape[0] == DEPTH
    xf = x.reshape(bsz * seq, d)
    ada = _ada(c, w_ada, b_ada)
    fn = final_norm.reshape(1, d)
    o_uv, o_z, o_xbc, o_dt, o_gate = 1024, 3072, 5120, 9216, 9248
    for l in range(DEPTH):
        mod = ada[l].reshape(bsz * N_ADA, 1, d)
        wl = w_in[l]
        w_main = jnp.concatenate(
            [wl[:, o_xbc:o_dt], wl[:, o_gate:], wl[:, o_uv:o_z], wl[:, o_z:o_xbc], wl[:, :o_uv]], axis=1
        ).astype(bf16)
        w_dt = jnp.pad(wl[:, o_dt:o_gate], ((0, 0), (0, DT_PAD - SSM_HEADS))).astype(bf16)
        proj, dtr = _inproj(xf, mod, w_main, w_dt)
        mixed_a = _pool(proj, pool_w[l].astype(bf16), pool_scale[l].reshape(1, POOL_WIDTH))
        gated_b = _gmlp(proj, gmlp_ln_g[l].reshape(1, GMLP_WIDTH), gmlp_ln_b[l].reshape(1, GMLP_WIDTH),
                        gmlp_ws[l], gmlp_bs[l].reshape(GMLP_HEADS, CHUNK, 1))
        yn = _ssd(proj, dtr, conv_w[l], conv_b[l].reshape(1, SSM_CONV_DIM), _pad_lanes(dt_bias[l]),
                  _pad_lanes(a_log[l]), jnp.repeat(d_skip[l], SSM_HEAD_DIM).reshape(1, SSM_INNER),
                  ssm_norm[l].reshape(1, SSM_INNER), bsz)
        merged = _merge(mixed_a, gated_b, yn, proj, w_pool_out[l].astype(bf16), w_gmlp_out[l].astype(bf16),
                        w_ssm_out[l].astype(bf16))
        xf = _wo(merged, w_o[l].astype(bf16), xf, mod)
        if l == 0:
            w_up_b, w_down_b = _to_bf16(w_up[l]), _to_bf16(w_down[l])
        else:
            w_up_b, w_down_b = w_up[l].astype(bf16), w_down[l].astype(bf16)
        xf = _mlp(xf, mod, w_up_b, w_down_b, fn, final_norm=(l == DEPTH - 1))
    return xf.reshape(bsz, seq, d)
```

```python
import functools

import jax
import jax.numpy as jnp
from jax import lax
from jax.experimental import pallas as pl
from jax.experimental.pallas import tpu as pltpu

f32 = jnp.float32
bf16 = jnp.bfloat16

D_MODEL = 2048
SEQ = 2048
DEPTH = 2
POOL_WIDTH = 1024
POOL_WINDOWS = (2, 4, 8, 16)
POOL_GROUP_DIM = 256
POOL_HALO = 16
GMLP_WIDTH = 1024
GMLP_HEADS = 8
GMLP_HEAD_DIM = 128
CHUNK = 128
SSM_INNER = 2048
SSM_HEAD_DIM = 64
SSM_HEADS = 32
SSM_GROUPS = 8
SSM_STATE = 128
SSM_CONV = 4
CONV_HALO = 8
SSM_CONV_DIM = 4096
GROUP_COLS = SSM_INNER // SSM_GROUPS
N_ADA = 6
D_FF = 8192
RMS_EPS = 1e-6
LN_EPS = 1e-5
LANES = 128
DT_PAD = LANES

XBC_OFF = 0
GATE_OFF = 4096
UV_OFF = 10240
Z_OFF = 12288
P_OFF = 14336
N_MAIN = 15360

VMEM_LIMIT = 56 * 1024 * 1024

SH1, SC1, G1, SH2, SC2, G2 = range(N_ADA)


def _params(*sem):
    return pltpu.CompilerParams(dimension_semantics=sem, vmem_limit_bytes=VMEM_LIMIT)


def _split_bf16(x, n):
    parts, r = [], x
    for _ in range(n):
        p = r.astype(bf16)
        parts.append(p)
        r = r - p.astype(f32)
    return parts


def _mod_rmsnorm(x_ref, sh_ref, sc_ref, h_ref, rows=256):
    sh = sh_ref[0]
    sc1p = 1.0 + sc_ref[0]

    def body(r, carry):
        rs = pl.ds(pl.multiple_of(r * rows, rows), rows)
        xf = x_ref[rs, :]
        ms = jnp.mean(xf * xf, axis=-1, keepdims=True)
        h_ref[rs, :] = ((xf * lax.rsqrt(ms + RMS_EPS)) * sc1p + sh).astype(h_ref.dtype)
        return carry

    lax.fori_loop(0, x_ref.shape[0] // rows, body, 0)


def _ada_kernel(c_ref, w_ref, b_ref, o_ref):
    c_act = jax.nn.silu(c_ref[...]).astype(bf16)
    part = jnp.dot(c_act, w_ref[0].astype(bf16), preferred_element_type=f32)

    @pl.when(pl.program_id(0) == 0)
    def _():
        o_ref[...] = part + b_ref[0]

    @pl.when(pl.program_id(0) > 0)
    def _():
        o_ref[...] += part


def _ada(c, w_ada, b_ada, layer, tk):
    depth, d, n = w_ada.shape
    bsz = c.shape[0]
    return pl.pallas_call(
        _ada_kernel,
        grid=(d // tk,),
        in_specs=[
            pl.BlockSpec((bsz, tk), lambda k: (0, k)),
            pl.BlockSpec((1, tk, n), lambda k: (layer, k, 0)),
            pl.BlockSpec((1, 1, n), lambda k: (layer, 0, 0)),
        ],
        out_specs=pl.BlockSpec((bsz, n), lambda k: (0, 0)),
        out_shape=jax.ShapeDtypeStruct((bsz, n), f32),
        compiler_params=_params("arbitrary"),
        name="ada",
    )(c, w_ada, b_ada.reshape(depth, 1, n))


def _inproj_kernel(x_ref, sh_ref, sc_ref, w_ref, wdt_ref, o_ref, odt_ref, h_ref):
    @pl.when(pl.program_id(1) == 0)
    def _():
        _mod_rmsnorm(x_ref, sh_ref, sc_ref, h_ref)
        odt_ref[...] = jnp.dot(h_ref[...], wdt_ref[...], preferred_element_type=f32)

    o_ref[...] = jnp.dot(h_ref[...], w_ref[...], preferred_element_type=f32)


def _inproj(xf, mod, w_main, w_dt, tm=1024, tn=1024):
    t, d = xf.shape
    tiles_per_seq = SEQ // tm
    return pl.pallas_call(
        _inproj_kernel,
        grid=(t // tm, N_MAIN // tn),
        in_specs=[
            pl.BlockSpec((tm, d), lambda i, j: (i, 0)),
            pl.BlockSpec((1, 1, d), lambda i, j: ((i // tiles_per_seq) * N_ADA + SH1, 0, 0)),
            pl.BlockSpec((1, 1, d), lambda i, j: ((i // tiles_per_seq) * N_ADA + SC1, 0, 0)),
            pl.BlockSpec((d, tn), lambda i, j: (0, j)),
            pl.BlockSpec((d, DT_PAD), lambda i, j: (0, 0)),
        ],
        out_specs=[
            pl.BlockSpec((tm, tn), lambda i, j: (i, j)),
            pl.BlockSpec((tm, DT_PAD), lambda i, j: (i, 0)),
        ],
        out_shape=[
            jax.ShapeDtypeStruct((t, N_MAIN), f32),
            jax.ShapeDtypeStruct((t, DT_PAD), f32),
        ],
        scratch_shapes=[pltpu.VMEM((tm, d), bf16)],
        compiler_params=_params("parallel", "arbitrary"),
        name="inproj",
    )(xf, mod, mod, w_main, w_dt)


def _pool_kernel(p_ref, halo_ref, pw_ref, scale_ref, o_ref, *, tiles_per_seq):
    ts = p_ref.shape[0]
    tile_in_seq = pl.program_id(0) % tiles_per_seq
    first = tile_in_seq == 0
    pos = lax.broadcasted_iota(jnp.int32, (ts, 1), 0) + tile_in_seq * ts
    for g, w in enumerate(POOL_WINDOWS):
        cols = slice(g * POOL_GROUP_DIM, (g + 1) * POOL_GROUP_DIM)
        pg = p_ref[:, cols]
        halo = jnp.where(first, 0.0, halo_ref[:, cols])
        s = jnp.concatenate([halo, pg], axis=0)
        k = 1
        while k < w:
            s = s + pltpu.roll(s, k, axis=0)
            k *= 2
        count = jnp.minimum(pos + 1, w).astype(f32)
        pooled = s[POOL_HALO:, :] / count - pg
        mixed = jnp.dot(pooled.astype(bf16), pw_ref[g], preferred_element_type=f32)
        o_ref[:, cols] = (mixed * scale_ref[:, cols]).astype(o_ref.dtype)


def _pool(proj, pool_w, pool_scale, ts=512):
    t = proj.shape[0]
    halo_blocks = ts // POOL_HALO
    p_blk = P_OFF // POOL_WIDTH
    return pl.pallas_call(
        functools.partial(_pool_kernel, tiles_per_seq=SEQ // ts),
        grid=(t // ts,),
        in_specs=[
            pl.BlockSpec((ts, POOL_WIDTH), lambda i: (i, p_blk)),
            pl.BlockSpec((POOL_HALO, POOL_WIDTH), lambda i: (jnp.maximum(i * halo_blocks - 1, 0), p_blk)),
            pl.BlockSpec(pool_w.shape, lambda i: (0, 0, 0)),
            pl.BlockSpec((1, POOL_WIDTH), lambda i: (0, 0)),
        ],
        out_specs=pl.BlockSpec((ts, POOL_WIDTH), lambda i: (i, 0)),
        out_shape=jax.ShapeDtypeStruct((t, POOL_WIDTH), bf16),
        compiler_params=_params("parallel"),
        name="pool",
    )(proj, proj, pool_w, pool_scale)


def _gelu(x):
    return 0.5 * x * (1.0 + lax.erf(x * 0.7071067811865476))


def _gmlp_kernel(u_ref, v_ref, g_ref, b_ref, ws_ref, bs_ref, o_ref, vn_ref):
    tg = u_ref.shape[0]
    v = _gelu(v_ref[...])
    mu = jnp.mean(v, axis=-1, keepdims=True)
    dv = v - mu
    var = jnp.mean(dv * dv, axis=-1, keepdims=True)
    vn_ref[...] = ((dv * lax.rsqrt(var + LN_EPS)) * g_ref[...] + b_ref[...]).astype(vn_ref.dtype)
    ri = lax.broadcasted_iota(jnp.int32, (CHUNK, CHUNK), 0)
    ci = lax.broadcasted_iota(jnp.int32, (CHUNK, CHUNK), 1)
    causal = ri >= ci
    for h in range(GMLP_HEADS):
        cols = slice(h * GMLP_HEAD_DIM, (h + 1) * GMLP_HEAD_DIM)
        wm = jnp.where(causal, ws_ref[h], 0.0).astype(bf16)
        bias = bs_ref[h]
        for c in range(tg // CHUNK):
            rows = slice(c * CHUNK, (c + 1) * CHUNK)
            mixed = jnp.dot(wm, vn_ref[rows, cols], preferred_element_type=f32) + bias
            o_ref[rows, cols] = (_gelu(u_ref[rows, cols]) * mixed).astype(o_ref.dtype)


def _gmlp(proj, ln_g, ln_b, ws, bs, tg=512):
    t = proj.shape[0]
    u_blk = UV_OFF // GMLP_WIDTH
    return pl.pallas_call(
        _gmlp_kernel,
        grid=(t // tg,),
        in_specs=[
            pl.BlockSpec((tg, GMLP_WIDTH), lambda i: (i, u_blk)),
            pl.BlockSpec((tg, GMLP_WIDTH), lambda i: (i, u_blk + 1)),
            pl.BlockSpec((1, GMLP_WIDTH), lambda i: (0, 0)),
            pl.BlockSpec((1, GMLP_WIDTH), lambda i: (0, 0)),
            pl.BlockSpec((GMLP_HEADS, CHUNK, CHUNK), lambda i: (0, 0, 0)),
            pl.BlockSpec((GMLP_HEADS, CHUNK, 1), lambda i: (0, 0, 0)),
        ],
        out_specs=pl.BlockSpec((tg, GMLP_WIDTH), lambda i: (i, 0)),
        out_shape=jax.ShapeDtypeStruct((t, GMLP_WIDTH), bf16),
        scratch_shapes=[pltpu.VMEM((tg, GMLP_WIDTH), bf16)],
        compiler_params=_params("parallel"),
        name="gmlp",
    )(proj, proj, ln_g, ln_b, ws, bs)


def _ssd_chunk(rows, dtr_ref, dtb_ref, alog_ref, z_ref, dskip_ref, ng_ref, o_ref, state_ref, xs_ref,
               bm_ref, cm_ref, wdt_ref, eout_ref, consts):
    causal, tril, expand, low_half = consts
    dt = jax.nn.softplus(dtr_ref[rows, :] + dtb_ref[...])
    da = dt * (-jnp.exp(alog_ref[...]))
    cs3 = jnp.dot(tril, jnp.concatenate(_split_bf16(da, 3), axis=1), preferred_element_type=f32)
    a_cs = cs3[:, :LANES] + cs3[:, LANES:2 * LANES] + cs3[:, 2 * LANES:]
    a_last = a_cs[CHUNK - 1:CHUNK, :]
    wdt = dt * jnp.exp(a_last - a_cs)
    e_out = jnp.exp(a_cs)
    a_cs_t = a_cs.T
    dt_t = dt.T
    pieces = jnp.concatenate(_split_bf16(wdt, 3) + _split_bf16(e_out, 3), axis=0)
    ex = jnp.dot(pieces, expand, preferred_element_type=f32)
    wdt_ref[rows, :] = ex[:CHUNK] + ex[CHUNK:2 * CHUNK] + ex[2 * CHUNK:3 * CHUNK]
    eout_ref[rows, :] = ex[3 * CHUNK:4 * CHUNK] + ex[4 * CHUNK:5 * CHUNK] + ex[5 * CHUNK:]
    last = rows.start + CHUNK - 1

    for g in range(SSM_GROUPS):
        gcols = slice(g * GROUP_COLS, (g + 1) * GROUP_COLS)
        ncols = slice(g * SSM_STATE, (g + 1) * SSM_STATE)
        bg = bm_ref[rows, ncols]
        cg = cm_ref[rows, ncols]
        cb = lax.dot_general(cg, bg, (((1,), (1,)), ((), ())), preferred_element_type=f32)
        y_pairs = []
        for pr in range(2):
            scores = []
            for hh in range(2):
                h = 4 * g + 2 * pr + hh
                diff = a_cs[:, h:h + 1] - a_cs_t[h:h + 1, :]
                decay = jnp.exp(jnp.where(causal, diff, -jnp.inf))
                scores.append((cb * decay * dt_t[h:h + 1, :]).astype(bf16))
            lhs = jnp.concatenate(scores, axis=1)
            xp = xs_ref[rows, g * GROUP_COLS + pr * LANES:g * GROUP_COLS + (pr + 1) * LANES]
            rhs = jnp.concatenate([jnp.where(low_half, xp, 0.0), jnp.where(low_half, 0.0, xp)], axis=0)
            y_pairs.append(jnp.dot(lhs, rhs.astype(bf16), preferred_element_type=f32))
        y = jnp.concatenate(y_pairs, axis=1)
        xs_g = xs_ref[rows, gcols]
        xdw = (xs_g * wdt_ref[rows, gcols]).astype(bf16)
        st_new = lax.dot_general(bg, xdw, (((0,), (0,)), ((), ())), preferred_element_type=f32)
        prev = state_ref[g]
        y = y + jnp.dot(cg, prev.astype(bf16), preferred_element_type=f32) * eout_ref[rows, gcols]
        state_ref[g] = prev * eout_ref[last:last + 1, gcols] + st_new
        y = y + dskip_ref[:, gcols] * xs_g
        y = y * jax.nn.silu(z_ref[rows, gcols])
        y = y * lax.rsqrt(jnp.mean(y * y, axis=-1, keepdims=True) + RMS_EPS)
        o_ref[rows, gcols] = (y * ng_ref[:, gcols]).astype(o_ref.dtype)


def _ssd_kernel(xbc_ref, halo_ref, z_ref, dtr_ref, cw_ref, cb_ref, dtb_ref, alog_ref, dskip_ref,
                ng_ref, o_ref, state_ref, xs_ref, bm_ref, cm_ref, wdt_ref, eout_ref):
    step = pl.program_id(1)

    @pl.when(step == 0)
    def _():
        state_ref[...] = jnp.zeros_like(state_ref)

    slab, rc = 256, 64
    n_state = SSM_GROUPS * SSM_STATE
    for s in range(SSM_CONV_DIM // slab):
        cols = slice(s * slab, (s + 1) * slab)
        w = cw_ref[:, cols]
        bias = cb_ref[:, cols]
        for r in range(xbc_ref.shape[0] // rc):
            rows = slice(r * rc, (r + 1) * rc)
            if r == 0:
                halo = jnp.where(step == 0, 0.0, halo_ref[:, cols])
            else:
                halo = xbc_ref[r * rc - CONV_HALO:r * rc, cols]
            cur = xbc_ref[rows, cols]
            e = jnp.concatenate([halo, cur], axis=0)
            acc = cur * w[SSM_CONV - 1:SSM_CONV, :]
            for k in range(1, SSM_CONV):
                acc = acc + pltpu.roll(e, k, axis=0)[CONV_HALO:, :] * w[SSM_CONV - 1 - k:SSM_CONV - k, :]
            act = jax.nn.silu(acc + bias)
            if s * slab < SSM_INNER:
                xs_ref[rows, cols] = act
            elif s * slab < SSM_INNER + n_state:
                off = s * slab - SSM_INNER
                bm_ref[rows, off:off + slab] = act.astype(bf16)
            else:
                off = s * slab - SSM_INNER - n_state
                cm_ref[rows, off:off + slab] = act.astype(bf16)

    ri = lax.broadcasted_iota(jnp.int32, (CHUNK, CHUNK), 0)
    ci = lax.broadcasted_iota(jnp.int32, (CHUNK, CHUNK), 1)
    causal = ri >= ci
    tril = jnp.where(causal, 1.0, 0.0).astype(bf16)
    hi = lax.broadcasted_iota(jnp.int32, (LANES, SSM_INNER), 0)
    ch = lax.broadcasted_iota(jnp.int32, (LANES, SSM_INNER), 1)
    expand = jnp.where(hi == ch // SSM_HEAD_DIM, 1.0, 0.0).astype(bf16)
    low_half = lax.broadcasted_iota(jnp.int32, (CHUNK, LANES), 1) < SSM_HEAD_DIM
    consts = (causal, tril, expand, low_half)
    for cc in range(xbc_ref.shape[0] // CHUNK):
        _ssd_chunk(slice(cc * CHUNK, (cc + 1) * CHUNK), dtr_ref, dtb_ref, alog_ref, z_ref, dskip_ref,
                   ng_ref, o_ref, state_ref, xs_ref, bm_ref, cm_ref, wdt_ref, eout_ref, consts)


def _ssd(proj, dtr, conv_w, conv_b, dt_bias, a_log, d_skip, norm_g, bsz, chunks_per_step=2):
    t = proj.shape[0]
    rb = chunks_per_step * CHUNK
    steps = SEQ // rb
    halo_blocks = rb // CONV_HALO
    row = lambda b, c: b * steps + c
    const = lambda b, c: (0, 0)
    return pl.pallas_call(
        _ssd_kernel,
        grid=(bsz, steps),
        in_specs=[
            pl.BlockSpec((rb, SSM_CONV_DIM), lambda b, c: (row(b, c), XBC_OFF // SSM_CONV_DIM)),
            pl.BlockSpec((CONV_HALO, SSM_CONV_DIM),
                         lambda b, c: (jnp.maximum(row(b, c) * halo_blocks - 1, 0), XBC_OFF // SSM_CONV_DIM)),
            pl.BlockSpec((rb, SSM_INNER), lambda b, c: (row(b, c), Z_OFF // SSM_INNER)),
            pl.BlockSpec((rb, DT_PAD), lambda b, c: (row(b, c), 0)),
            pl.BlockSpec((SSM_CONV, SSM_CONV_DIM), const),
            pl.BlockSpec((1, SSM_CONV_DIM), const),
            pl.BlockSpec((1, DT_PAD), const),
            pl.BlockSpec((1, DT_PAD), const),
            pl.BlockSpec((1, SSM_INNER), const),
            pl.BlockSpec((1, SSM_INNER), const),
        ],
        out_specs=pl.BlockSpec((rb, SSM_INNER), lambda b, c: (row(b, c), 0)),
        out_shape=jax.ShapeDtypeStruct((t, SSM_INNER), bf16),
        scratch_shapes=[
            pltpu.VMEM((SSM_GROUPS, SSM_STATE, GROUP_COLS), f32),
            pltpu.VMEM((rb, SSM_INNER), f32),
            pltpu.VMEM((rb, SSM_GROUPS * SSM_STATE), bf16),
            pltpu.VMEM((rb, SSM_GROUPS * SSM_STATE), bf16),
            pltpu.VMEM((rb, SSM_INNER), f32),
            pltpu.VMEM((rb, SSM_INNER), f32),
        ],
        compiler_params=_params("parallel", "arbitrary"),
        name="ssd",
    )(proj, proj, proj, dtr, conv_w, conv_b, dt_bias, a_log, d_skip, norm_g)


def _merge_kernel(a_ref, b_ref, c_ref, wa_ref, wb_ref, wc_ref, ga_ref, gb_ref, gc_ref, o_ref):
    ya = jnp.dot(a_ref[...], wa_ref[...], preferred_element_type=f32)
    yb = jnp.dot(b_ref[...], wb_ref[...], preferred_element_type=f32)
    yc = jnp.dot(c_ref[...], wc_ref[...], preferred_element_type=f32)
    merged = (jax.nn.sigmoid(ga_ref[...]) * ya + jax.nn.sigmoid(gb_ref[...]) * yb
              + jax.nn.sigmoid(gc_ref[...]) * yc)
    o_ref[...] = merged.astype(o_ref.dtype)


def _tile_grid(n_row, n_col, cols_outer):
    if cols_outer:
        return (n_col, n_row), lambda f: (lambda j, i: f(i, j))
    return (n_row, n_col), lambda f: f


def _merge(ma, gb_, yn, proj, wa, wb, wc, tm, tn, cols_outer):
    t = proj.shape[0]
    d = D_MODEL
    gate_blk = GATE_OFF // tn
    per_gate = d // tn
    grid, im = _tile_grid(t // tm, d // tn, cols_outer)
    return pl.pallas_call(
        _merge_kernel,
        grid=grid,
        in_specs=[
            pl.BlockSpec((tm, POOL_WIDTH), im(lambda i, j: (i, 0))),
            pl.BlockSpec((tm, GMLP_WIDTH), im(lambda i, j: (i, 0))),
            pl.BlockSpec((tm, SSM_INNER), im(lambda i, j: (i, 0))),
            pl.BlockSpec((POOL_WIDTH, tn), im(lambda i, j: (0, j))),
            pl.BlockSpec((GMLP_WIDTH, tn), im(lambda i, j: (0, j))),
            pl.BlockSpec((SSM_INNER, tn), im(lambda i, j: (0, j))),
            pl.BlockSpec((tm, tn), im(lambda i, j: (i, gate_blk + j))),
            pl.BlockSpec((tm, tn), im(lambda i, j: (i, gate_blk + per_gate + j))),
            pl.BlockSpec((tm, tn), im(lambda i, j: (i, gate_blk + 2 * per_gate + j))),
        ],
        out_specs=pl.BlockSpec((tm, tn), im(lambda i, j: (i, j))),
        out_shape=jax.ShapeDtypeStruct((t, d), bf16),
        compiler_params=_params("arbitrary", "arbitrary"),
        name="merge",
    )(ma, gb_, yn, wa, wb, wc, proj, proj, proj)


def _wo_kernel(m_ref, w_ref, x_ref, g_ref, o_ref):
    o_ref[...] = x_ref[...] + g_ref[0] * jnp.dot(m_ref[...], w_ref[...], preferred_element_type=f32)


def _wo(merged, w_o, xf, mod, tm, tn, cols_outer):
    t, d = xf.shape
    tiles_per_batch = SEQ // tm
    grid, im = _tile_grid(t // tm, d // tn, cols_outer)
    return pl.pallas_call(
        _wo_kernel,
        grid=grid,
        in_specs=[
            pl.BlockSpec((tm, d), im(lambda i, j: (i, 0))),
            pl.BlockSpec((d, tn), im(lambda i, j: (0, j))),
            pl.BlockSpec((tm, tn), im(lambda i, j: (i, j))),
            pl.BlockSpec((1, 1, tn), im(lambda i, j: ((i // tiles_per_batch) * N_ADA + G1, 0, j))),
        ],
        out_specs=pl.BlockSpec((tm, tn), im(lambda i, j: (i, j))),
        out_shape=jax.ShapeDtypeStruct((t, d), f32),
        compiler_params=_params("arbitrary", "arbitrary"),
        name="wo",
    )(merged, w_o, xf, mod)


def _mlp_kernel(x_ref, sh_ref, sc_ref, g_ref, wu_ref, wd_ref, fn_ref, o_ref, h_ref, *, final_norm):
    f = pl.program_id(1)

    @pl.when(f == 0)
    def _():
        _mod_rmsnorm(x_ref, sh_ref, sc_ref, h_ref)
        o_ref[...] = jnp.zeros_like(o_ref)

    u = jnp.dot(h_ref[...], wu_ref[...], preferred_element_type=f32)
    u = jnp.square(jnp.maximum(u, 0.0)).astype(bf16)
    o_ref[...] += jnp.dot(u, wd_ref[...], preferred_element_type=f32)

    @pl.when(f == pl.num_programs(1) - 1)
    def _():
        rows = 256
        gate = g_ref[0]

        def body(r, carry):
            rs = pl.ds(pl.multiple_of(r * rows, rows), rows)
            xn = x_ref[rs, :] + gate * o_ref[rs, :]
            if final_norm:
                ms = jnp.mean(xn * xn, axis=-1, keepdims=True)
                xn = (xn * lax.rsqrt(ms + RMS_EPS)) * fn_ref[...]
            o_ref[rs, :] = xn
            return carry

        lax.fori_loop(0, x_ref.shape[0] // rows, body, 0)


def _mlp(xf, mod, w_up, w_down, fn, final_norm, tm=1024, tf=512):
    t, d = xf.shape
    tiles_per_batch = SEQ // tm
    mod_spec = lambda k: pl.BlockSpec((1, 1, d), lambda i, f: ((i // tiles_per_batch) * N_ADA + k, 0, 0))
    return pl.pallas_call(
        functools.partial(_mlp_kernel, final_norm=final_norm),
        grid=(t // tm, D_FF // tf),
        in_specs=[
            pl.BlockSpec((tm, d), lambda i, f: (i, 0)),
            mod_spec(SH2),
            mod_spec(SC2),
            mod_spec(G2),
            pl.BlockSpec((d, tf), lambda i, f: (0, f)),
            pl.BlockSpec((tf, d), lambda i, f: (f, 0)),
            pl.BlockSpec((1, d), lambda i, f: (0, 0)),
        ],
        out_specs=pl.BlockSpec((tm, d), lambda i, f: (i, 0)),
        out_shape=jax.ShapeDtypeStruct((t, d), f32),
        scratch_shapes=[pltpu.VMEM((tm, d), bf16)],
        compiler_params=_params("parallel", "arbitrary"),
        name="mlp",
    )(xf, mod, mod, mod, w_up, w_down, fn)


W_UV, W_Z, W_XBC, W_DT, W_GATE, W_END = 1024, 3072, 5120, 9216, 9248, 15392
WIN_TILE = 512
GATE_SHIFT = W_GATE - W_DT
GATE_TILE_LO = (GATE_OFF - XBC_OFF) // WIN_TILE
GATE_TILE_HI = (UV_OFF - XBC_OFF) // WIN_TILE


def _win_src_tile(j):
    return jnp.where(j < GATE_TILE_HI, j + W_XBC // WIN_TILE,
                     jnp.where(j < (P_OFF - XBC_OFF) // WIN_TILE, j + (W_UV - UV_OFF) // WIN_TILE,
                               j - (P_OFF - XBC_OFF) // WIN_TILE))


def _win_kernel(a_ref, b_ref, tail_ref, o_ref):
    j = pl.program_id(0)
    is_gate = jnp.logical_and(j >= GATE_TILE_LO, j < GATE_TILE_HI)

    @pl.when(is_gate)
    def _():
        nxt = jnp.where(j == GATE_TILE_HI - 1, tail_ref[...], b_ref[0])
        groups = [a_ref[0, :, q * LANES:(q + 1) * LANES] for q in range(WIN_TILE // LANES)] + [nxt]
        rolled = [pltpu.roll(g, LANES - GATE_SHIFT, axis=1) for g in groups]
        keep = lax.broadcasted_iota(jnp.int32, rolled[0].shape, 1) < LANES - GATE_SHIFT
        for q in range(WIN_TILE // LANES):
            o_ref[:, q * LANES:(q + 1) * LANES] = jnp.where(keep, rolled[q], rolled[q + 1]).astype(o_ref.dtype)

    @pl.when(jnp.logical_not(is_gate))
    def _():
        o_ref[...] = a_ref[0].astype(o_ref.dtype)


def _win_prep(w_in, layer):
    _, d, n = w_in.shape
    assert n == W_END and (W_END - W_GATE) == UV_OFF - GATE_OFF
    tail = jnp.pad(w_in[layer, :, n - GATE_SHIFT:], ((0, 0), (0, LANES - GATE_SHIFT)))
    sub = WIN_TILE // LANES
    last_full = n // LANES - 1

    def next_group(j):
        jj = jnp.clip(j, GATE_TILE_LO, GATE_TILE_HI - 1)
        return jnp.minimum((_win_src_tile(jj) + 1) * sub, last_full)

    return pl.pallas_call(
        _win_kernel,
        grid=(N_MAIN // WIN_TILE,),
        in_specs=[
            pl.BlockSpec((1, d, WIN_TILE), lambda j: (layer, 0, _win_src_tile(j))),
            pl.BlockSpec((1, d, LANES), lambda j: (layer, 0, next_group(j))),
            pl.BlockSpec((d, LANES), lambda j: (0, 0)),
        ],
        out_specs=pl.BlockSpec((d, WIN_TILE), lambda j: (0, j)),
        out_shape=jax.ShapeDtypeStruct((d, N_MAIN), bf16),
        compiler_params=_params("parallel"),
        name="winprep",
    )(w_in, w_in, tail)


def _pad_lanes(v):
    return jnp.pad(v.astype(f32), (0, DT_PAD - v.shape[0])).reshape(1, DT_PAD)


def kernel(x, c, w_ada, b_ada, w_in, pool_w, pool_scale, gmlp_ln_g, gmlp_ln_b, gmlp_ws, gmlp_bs, conv_w, conv_b, dt_bias, a_log, d_skip, ssm_norm, w_pool_out, w_gmlp_out, w_ssm_out, w_o, w_up, w_down, final_norm):
    bsz, seq, d = x.shape
    assert (seq, d) == (SEQ, D_MODEL) and w_ada.shape[0] == DEPTH
    xf = x.reshape(bsz * seq, d)
    fn = final_norm.reshape(1, d)
    for l in range(DEPTH):
        mod = _ada(c, w_ada, b_ada, l, tk=128 if l == 0 else 256).reshape(bsz * N_ADA, 1, d)
        w_main = _win_prep(w_in, l)
        w_dt = jnp.pad(w_in[l, :, W_DT:W_GATE], ((0, 0), (0, DT_PAD - SSM_HEADS))).astype(bf16)
        proj, dtr = _inproj(xf, mod, w_main, w_dt)
        mixed_a = _pool(proj, pool_w[l].astype(bf16), pool_scale[l].reshape(1, POOL_WIDTH))
        gated_b = _gmlp(proj, gmlp_ln_g[l].reshape(1, GMLP_WIDTH), gmlp_ln_b[l].reshape(1, GMLP_WIDTH),
                        gmlp_ws[l], gmlp_bs[l].reshape(GMLP_HEADS, CHUNK, 1))
        yn = _ssd(proj, dtr, conv_w[l], conv_b[l].reshape(1, SSM_CONV_DIM), _pad_lanes(dt_bias[l]),
                  _pad_lanes(a_log[l]), jnp.repeat(d_skip[l], SSM_HEAD_DIM).reshape(1, SSM_INNER),
                  ssm_norm[l].reshape(1, SSM_INNER), bsz)
        tiling = dict(tm=512, tn=1024, cols_outer=True) if l == 0 else dict(tm=1024, tn=512, cols_outer=False)
        merged = _merge(mixed_a, gated_b, yn, proj, w_pool_out[l].astype(bf16), w_gmlp_out[l].astype(bf16),
                        w_ssm_out[l].astype(bf16), **tiling)
        xf = _wo(merged, w_o[l].astype(bf16), xf, mod, **tiling)
        xf = _mlp(xf, mod, w_up[l].astype(bf16), w_down[l].astype(bf16), fn, final_norm=(l == DEPTH - 1))
    return xf.reshape(bsz, seq, d)
```

```python
import functools

import jax
import jax.numpy as jnp
from jax import lax
from jax.experimental import pallas as pl
from jax.experimental.pallas import tpu as pltpu

f32 = jnp.float32
bf16 = jnp.bfloat16

D_MODEL = 2048
SEQ = 2048
DEPTH = 2
POOL_WIDTH = 1024
POOL_WINDOWS = (2, 4, 8, 16)
POOL_GROUP_DIM = 256
POOL_HALO = 16
GMLP_WIDTH = 1024
GMLP_HEADS = 8
GMLP_HEAD_DIM = 128
CHUNK = 128
SSM_INNER = 2048
SSM_HEAD_DIM = 64
SSM_HEADS = 32
SSM_GROUPS = 8
SSM_STATE = 128
SSM_CONV = 4
CONV_HALO = 8
SSM_CONV_DIM = 4096
GROUP_COLS = SSM_INNER // SSM_GROUPS
N_ADA = 6
D_FF = 8192
RMS_EPS = 1e-6
LN_EPS = 1e-5
LANES = 128
DT_PAD = LANES

XBC_OFF = 0
GATE_OFF = 4096
UV_OFF = 10240
Z_OFF = 12288
P_OFF = 14336
N_MAIN = 15360
W_UV, W_Z, W_XBC, W_DT, W_GATE, W_END = 1024, 3072, 5120, 9216, 9248, 15392

VMEM_LIMIT = 56 * 1024 * 1024

SH1, SC1, G1, SH2, SC2, G2 = range(N_ADA)


def _params(*sem):
    return pltpu.CompilerParams(dimension_semantics=sem, vmem_limit_bytes=VMEM_LIMIT)


def _split_bf16(x, n):
    parts, r = [], x
    for _ in range(n):
        p = r.astype(bf16)
        parts.append(p)
        r = r - p.astype(f32)
    return parts


def _mod_rmsnorm(x_ref, sh_ref, sc_ref, h_ref, rows=256):
    sh = sh_ref[0]
    sc1p = 1.0 + sc_ref[0]

    def body(r, carry):
        rs = pl.ds(pl.multiple_of(r * rows, rows), rows)
        xf = x_ref[rs, :]
        ms = jnp.mean(xf * xf, axis=-1, keepdims=True)
        h_ref[rs, :] = ((xf * lax.rsqrt(ms + RMS_EPS)) * sc1p + sh).astype(h_ref.dtype)
        return carry

    lax.fori_loop(0, x_ref.shape[0] // rows, body, 0)


def _ada_kernel(c_ref, w_ref, b_ref, o_ref):
    c_act = jax.nn.silu(c_ref[...]).astype(bf16)
    part = jnp.dot(c_act, w_ref[0].astype(bf16), preferred_element_type=f32)

    @pl.when(pl.program_id(0) == 0)
    def _():
        o_ref[...] = part + b_ref[0]

    @pl.when(pl.program_id(0) > 0)
    def _():
        o_ref[...] += part


def _ada(c, w_ada, b_ada, layer, tk):
    depth, d, n = w_ada.shape
    bsz = c.shape[0]
    return pl.pallas_call(
        _ada_kernel,
        grid=(d // tk,),
        in_specs=[
            pl.BlockSpec((bsz, tk), lambda k: (0, k)),
            pl.BlockSpec((1, tk, n), lambda k: (layer, k, 0)),
            pl.BlockSpec((1, 1, n), lambda k: (layer, 0, 0)),
        ],
        out_specs=pl.BlockSpec((bsz, n), lambda k: (0, 0)),
        out_shape=jax.ShapeDtypeStruct((bsz, n), f32),
        compiler_params=_params("arbitrary"),
        name="ada",
    )(c, w_ada, b_ada.reshape(depth, 1, n))


def _inproj_kernel(x_ref, sh_ref, sc_ref, w_ref, wdt_ref, o_ref, odt_ref, h_ref):
    @pl.when(pl.program_id(1) == 0)
    def _():
        _mod_rmsnorm(x_ref, sh_ref, sc_ref, h_ref)
        dt_all = lax.dot_general(h_ref[...], wdt_ref[0].astype(bf16), (((1,), (1,)), ((), ())),
                                 preferred_element_type=f32)
        lane = lax.broadcasted_iota(jnp.int32, dt_all.shape, 1)
        odt_ref[...] = jnp.where(lane < SSM_HEADS, dt_all, 0.0)

    o_ref[...] = jnp.dot(h_ref[...], w_ref[...], preferred_element_type=f32)


def _inproj(xf, mod, w_main, w_in_t, layer, tm=1024, tn=1024):
    t, d = xf.shape
    tiles_per_seq = SEQ // tm
    return pl.pallas_call(
        _inproj_kernel,
        grid=(t // tm, N_MAIN // tn),
        in_specs=[
            pl.BlockSpec((tm, d), lambda i, j: (i, 0)),
            pl.BlockSpec((1, 1, d), lambda i, j: ((i // tiles_per_seq) * N_ADA + SH1, 0, 0)),
            pl.BlockSpec((1, 1, d), lambda i, j: ((i // tiles_per_seq) * N_ADA + SC1, 0, 0)),
            pl.BlockSpec((d, tn), lambda i, j: (0, j)),
            pl.BlockSpec((pl.Element(1), pl.Element(DT_PAD), pl.Element(d)), lambda i, j: (layer, W_DT, 0)),
        ],
        out_specs=[
            pl.BlockSpec((tm, tn), lambda i, j: (i, j)),
            pl.BlockSpec((tm, DT_PAD), lambda i, j: (i, 0)),
        ],
        out_shape=[
            jax.ShapeDtypeStruct((t, N_MAIN), f32),
            jax.ShapeDtypeStruct((t, DT_PAD), f32),
        ],
        scratch_shapes=[pltpu.VMEM((tm, d), bf16)],
        compiler_params=_params("parallel", "arbitrary"),
        name="inproj",
    )(xf, mod, mod, w_main, w_in_t)


def _pool_kernel(p_ref, halo_ref, pw_ref, scale_ref, o_ref, *, tiles_per_seq):
    ts = p_ref.shape[0]
    tile_in_seq = pl.program_id(0) % tiles_per_seq
    first = tile_in_seq == 0
    pos = lax.broadcasted_iota(jnp.int32, (ts, 1), 0) + tile_in_seq * ts
    for g, w in enumerate(POOL_WINDOWS):
        cols = slice(g * POOL_GROUP_DIM, (g + 1) * POOL_GROUP_DIM)
        pg = p_ref[:, cols]
        halo = jnp.where(first, 0.0, halo_ref[:, cols])
        s = jnp.concatenate([halo, pg], axis=0)
        k = 1
        while k < w:
            s = s + pltpu.roll(s, k, axis=0)
            k *= 2
        count = jnp.minimum(pos + 1, w).astype(f32)
        pooled = s[POOL_HALO:, :] / count - pg
        mixed = jnp.dot(pooled.astype(bf16), pw_ref[g], preferred_element_type=f32)
        o_ref[:, cols] = (mixed * scale_ref[:, cols]).astype(o_ref.dtype)


def _pool(proj, pool_w, pool_scale, ts=512):
    t = proj.shape[0]
    halo_blocks = ts // POOL_HALO
    p_blk = P_OFF // POOL_WIDTH
    return pl.pallas_call(
        functools.partial(_pool_kernel, tiles_per_seq=SEQ // ts),
        grid=(t // ts,),
        in_specs=[
            pl.BlockSpec((ts, POOL_WIDTH), lambda i: (i, p_blk)),
            pl.BlockSpec((POOL_HALO, POOL_WIDTH), lambda i: (jnp.maximum(i * halo_blocks - 1, 0), p_blk)),
            pl.BlockSpec(pool_w.shape, lambda i: (0, 0, 0)),
            pl.BlockSpec((1, POOL_WIDTH), lambda i: (0, 0)),
        ],
        out_specs=pl.BlockSpec((ts, POOL_WIDTH), lambda i: (i, 0)),
        out_shape=jax.ShapeDtypeStruct((t, POOL_WIDTH), bf16),
        compiler_params=_params("parallel"),
        name="pool",
    )(proj, proj, pool_w, pool_scale)


def _gelu(x):
    return 0.5 * x * (1.0 + lax.erf(x * 0.7071067811865476))


def _gmlp_kernel(u_ref, v_ref, g_ref, b_ref, ws_ref, bs_ref, o_ref, vn_ref):
    tg = u_ref.shape[0]
    v = _gelu(v_ref[...])
    mu = jnp.mean(v, axis=-1, keepdims=True)
    dv = v - mu
    var = jnp.mean(dv * dv, axis=-1, keepdims=True)
    vn_ref[...] = ((dv * lax.rsqrt(var + LN_EPS)) * g_ref[...] + b_ref[...]).astype(vn_ref.dtype)
    ri = lax.broadcasted_iota(jnp.int32, (CHUNK, CHUNK), 0)
    ci = lax.broadcasted_iota(jnp.int32, (CHUNK, CHUNK), 1)
    causal = ri >= ci
    for h in range(GMLP_HEADS):
        cols = slice(h * GMLP_HEAD_DIM, (h + 1) * GMLP_HEAD_DIM)
        wm = jnp.where(causal, ws_ref[h], 0.0).astype(bf16)
        bias = bs_ref[h]
        for c in range(tg // CHUNK):
            rows = slice(c * CHUNK, (c + 1) * CHUNK)
            mixed = jnp.dot(wm, vn_ref[rows, cols], preferred_element_type=f32) + bias
            o_ref[rows, cols] = (_gelu(u_ref[rows, cols]) * mixed).astype(o_ref.dtype)


def _gmlp(proj, ln_g, ln_b, ws, bs, tg=512):
    t = proj.shape[0]
    u_blk = UV_OFF // GMLP_WIDTH
    return pl.pallas_call(
        _gmlp_kernel,
        grid=(t // tg,),
        in_specs=[
            pl.BlockSpec((tg, GMLP_WIDTH), lambda i: (i, u_blk)),
            pl.BlockSpec((tg, GMLP_WIDTH), lambda i: (i, u_blk + 1)),
            pl.BlockSpec((1, GMLP_WIDTH), lambda i: (0, 0)),
            pl.BlockSpec((1, GMLP_WIDTH), lambda i: (0, 0)),
            pl.BlockSpec((GMLP_HEADS, CHUNK, CHUNK), lambda i: (0, 0, 0)),
            pl.BlockSpec((GMLP_HEADS, CHUNK, 1), lambda i: (0, 0, 0)),
        ],
        out_specs=pl.BlockSpec((tg, GMLP_WIDTH), lambda i: (i, 0)),
        out_shape=jax.ShapeDtypeStruct((t, GMLP_WIDTH), bf16),
        scratch_shapes=[pltpu.VMEM((tg, GMLP_WIDTH), bf16)],
        compiler_params=_params("parallel"),
        name="gmlp",
    )(proj, proj, ln_g, ln_b, ws, bs)


def _ssd_chunk(rows, dtr_ref, dtb_ref, alog_ref, z_ref, dskip_ref, ng_ref, o_ref, state_ref, xs_ref,
               bm_ref, cm_ref, wdt_ref, eout_ref, consts):
    causal, tril, expand, low_half = consts
    dt = jax.nn.softplus(dtr_ref[rows, :] + dtb_ref[...])
    da = dt * (-jnp.exp(alog_ref[...]))
    cs3 = jnp.dot(tril, jnp.concatenate(_split_bf16(da, 3), axis=1), preferred_element_type=f32)
    a_cs = cs3[:, :LANES] + cs3[:, LANES:2 * LANES] + cs3[:, 2 * LANES:]
    a_last = a_cs[CHUNK - 1:CHUNK, :]
    wdt = dt * jnp.exp(a_last - a_cs)
    e_out = jnp.exp(a_cs)
    a_cs_t = a_cs.T
    dt_t = dt.T
    pieces = jnp.concatenate(_split_bf16(wdt, 3) + _split_bf16(e_out, 3), axis=0)
    ex = jnp.dot(pieces, expand, preferred_element_type=f32)
    wdt_ref[rows, :] = ex[:CHUNK] + ex[CHUNK:2 * CHUNK] + ex[2 * CHUNK:3 * CHUNK]
    eout_ref[rows, :] = ex[3 * CHUNK:4 * CHUNK] + ex[4 * CHUNK:5 * CHUNK] + ex[5 * CHUNK:]
    last = rows.start + CHUNK - 1

    for g in range(SSM_GROUPS):
        gcols = slice(g * GROUP_COLS, (g + 1) * GROUP_COLS)
        ncols = slice(g * SSM_STATE, (g + 1) * SSM_STATE)
        bg = bm_ref[rows, ncols]
        cg = cm_ref[rows, ncols]
        cb = lax.dot_general(cg, bg, (((1,), (1,)), ((), ())), preferred_element_type=f32)
        y_pairs = []
        for pr in range(2):
            scores = []
            for hh in range(2):
                h = 4 * g + 2 * pr + hh
                diff = a_cs[:, h:h + 1] - a_cs_t[h:h + 1, :]
                decay = jnp.exp(jnp.where(causal, diff, -jnp.inf))
                scores.append((cb * decay * dt_t[h:h + 1, :]).astype(bf16))
            lhs = jnp.concatenate(scores, axis=1)
            xp = xs_ref[rows, g * GROUP_COLS + pr * LANES:g * GROUP_COLS + (pr + 1) * LANES]
            rhs = jnp.concatenate([jnp.where(low_half, xp, 0.0), jnp.where(low_half, 0.0, xp)], axis=0)
            y_pairs.append(jnp.dot(lhs, rhs.astype(bf16), preferred_element_type=f32))
        y = jnp.concatenate(y_pairs, axis=1)
        xs_g = xs_ref[rows, gcols]
        xdw = (xs_g * wdt_ref[rows, gcols]).astype(bf16)
        st_new = lax.dot_general(bg, xdw, (((0,), (0,)), ((), ())), preferred_element_type=f32)
        prev = state_ref[g]
        y = y + jnp.dot(cg, prev.astype(bf16), preferred_element_type=f32) * eout_ref[rows, gcols]
        state_ref[g] = prev * eout_ref[last:last + 1, gcols] + st_new
        y = y + dskip_ref[:, gcols] * xs_g
        y = y * jax.nn.silu(z_ref[rows, gcols])
        y = y * lax.rsqrt(jnp.mean(y * y, axis=-1, keepdims=True) + RMS_EPS)
        o_ref[rows, gcols] = (y * ng_ref[:, gcols]).astype(o_ref.dtype)


def _ssd_kernel(xbc_ref, halo_ref, z_ref, dtr_ref, cw_ref, cb_ref, dtb_ref, alog_ref, dskip_ref,
                ng_ref, o_ref, state_ref, xs_ref, bm_ref, cm_ref, wdt_ref, eout_ref):
    step = pl.program_id(1)

    @pl.when(step == 0)
    def _():
        state_ref[...] = jnp.zeros_like(state_ref)

    slab, rc = 256, 64
    n_state = SSM_GROUPS * SSM_STATE
    for s in range(SSM_CONV_DIM // slab):
        cols = slice(s * slab, (s + 1) * slab)
        w = cw_ref[:, cols]
        bias = cb_ref[:, cols]
        for r in range(xbc_ref.shape[0] // rc):
            rows = slice(r * rc, (r + 1) * rc)
            if r == 0:
                halo = jnp.where(step == 0, 0.0, halo_ref[:, cols])
            else:
                halo = xbc_ref[r * rc - CONV_HALO:r * rc, cols]
            cur = xbc_ref[rows, cols]
            e = jnp.concatenate([halo, cur], axis=0)
            acc = cur * w[SSM_CONV - 1:SSM_CONV, :]
            for k in range(1, SSM_CONV):
                acc = acc + pltpu.roll(e, k, axis=0)[CONV_HALO:, :] * w[SSM_CONV - 1 - k:SSM_CONV - k, :]
            act = jax.nn.silu(acc + bias)
            if s * slab < SSM_INNER:
                xs_ref[rows, cols] = act
            elif s * slab < SSM_INNER + n_state:
                off = s * slab - SSM_INNER
                bm_ref[rows, off:off + slab] = act.astype(bf16)
            else:
                off = s * slab - SSM_INNER - n_state
                cm_ref[rows, off:off + slab] = act.astype(bf16)

    ri = lax.broadcasted_iota(jnp.int32, (CHUNK, CHUNK), 0)
    ci = lax.broadcasted_iota(jnp.int32, (CHUNK, CHUNK), 1)
    causal = ri >= ci
    tril = jnp.where(causal, 1.0, 0.0).astype(bf16)
    hi = lax.broadcasted_iota(jnp.int32, (LANES, SSM_INNER), 0)
    ch = lax.broadcasted_iota(jnp.int32, (LANES, SSM_INNER), 1)
    expand = jnp.where(hi == ch // SSM_HEAD_DIM, 1.0, 0.0).astype(bf16)
    low_half = lax.broadcasted_iota(jnp.int32, (CHUNK, LANES), 1) < SSM_HEAD_DIM
    consts = (causal, tril, expand, low_half)
    for cc in range(xbc_ref.shape[0] // CHUNK):
        _ssd_chunk(slice(cc * CHUNK, (cc + 1) * CHUNK), dtr_ref, dtb_ref, alog_ref, z_ref, dskip_ref,
                   ng_ref, o_ref, state_ref, xs_ref, bm_ref, cm_ref, wdt_ref, eout_ref, consts)


def _ssd(proj, dtr, conv_w, conv_b, dt_bias, a_log, d_skip, norm_g, bsz, chunks_per_step=2):
    t = proj.shape[0]
    rb = chunks_per_step * CHUNK
    steps = SEQ // rb
    halo_blocks = rb // CONV_HALO
    row = lambda b, c: b * steps + c
    const = lambda b, c: (0, 0)
    return pl.pallas_call(
        _ssd_kernel,
        grid=(bsz, steps),
        in_specs=[
            pl.BlockSpec((rb, SSM_CONV_DIM), lambda b, c: (row(b, c), XBC_OFF // SSM_CONV_DIM)),
            pl.BlockSpec((CONV_HALO, SSM_CONV_DIM),
                         lambda b, c: (jnp.maximum(row(b, c) * halo_blocks - 1, 0), XBC_OFF // SSM_CONV_DIM)),
            pl.BlockSpec((rb, SSM_INNER), lambda b, c: (row(b, c), Z_OFF // SSM_INNER)),
            pl.BlockSpec((rb, DT_PAD), lambda b, c: (row(b, c), 0)),
            pl.BlockSpec((SSM_CONV, SSM_CONV_DIM), const),
            pl.BlockSpec((1, SSM_CONV_DIM), const),
            pl.BlockSpec((1, DT_PAD), const),
            pl.BlockSpec((1, DT_PAD), const),
            pl.BlockSpec((1, SSM_INNER), const),
            pl.BlockSpec((1, SSM_INNER), const),
        ],
        out_specs=pl.BlockSpec((rb, SSM_INNER), lambda b, c: (row(b, c), 0)),
        out_shape=jax.ShapeDtypeStruct((t, SSM_INNER), bf16),
        scratch_shapes=[
            pltpu.VMEM((SSM_GROUPS, SSM_STATE, GROUP_COLS), f32),
            pltpu.VMEM((rb, SSM_INNER), f32),
            pltpu.VMEM((rb, SSM_GROUPS * SSM_STATE), bf16),
            pltpu.VMEM((rb, SSM_GROUPS * SSM_STATE), bf16),
            pltpu.VMEM((rb, SSM_INNER), f32),
            pltpu.VMEM((rb, SSM_INNER), f32),
        ],
        compiler_params=_params("parallel", "arbitrary"),
        name="ssd",
    )(proj, proj, proj, dtr, conv_w, conv_b, dt_bias, a_log, d_skip, norm_g)


def _merge_kernel(a_ref, b_ref, c_ref, wa_ref, wb_ref, wc_ref, ga_ref, gb_ref, gc_ref, o_ref):
    ya = jnp.dot(a_ref[...], wa_ref[...], preferred_element_type=f32)
    yb = jnp.dot(b_ref[...], wb_ref[...], preferred_element_type=f32)
    yc = jnp.dot(c_ref[...], wc_ref[...], preferred_element_type=f32)
    merged = (jax.nn.sigmoid(ga_ref[...]) * ya + jax.nn.sigmoid(gb_ref[...]) * yb
              + jax.nn.sigmoid(gc_ref[...]) * yc)
    o_ref[...] = merged.astype(o_ref.dtype)


def _tile_grid(n_row, n_col, cols_outer):
    if cols_outer:
        return (n_col, n_row), lambda f: (lambda j, i: f(i, j))
    return (n_row, n_col), lambda f: f


def _merge(ma, gb_, yn, proj, wa, wb, wc, tm, tn, cols_outer):
    t = proj.shape[0]
    d = D_MODEL
    gate_blk = GATE_OFF // tn
    per_gate = d // tn
    grid, im = _tile_grid(t // tm, d // tn, cols_outer)
    return pl.pallas_call(
        _merge_kernel,
        grid=grid,
        in_specs=[
            pl.BlockSpec((tm, POOL_WIDTH), im(lambda i, j: (i, 0))),
            pl.BlockSpec((tm, GMLP_WIDTH), im(lambda i, j: (i, 0))),
            pl.BlockSpec((tm, SSM_INNER), im(lambda i, j: (i, 0))),
            pl.BlockSpec((POOL_WIDTH, tn), im(lambda i, j: (0, j))),
            pl.BlockSpec((GMLP_WIDTH, tn), im(lambda i, j: (0, j))),
            pl.BlockSpec((SSM_INNER, tn), im(lambda i, j: (0, j))),
            pl.BlockSpec((tm, tn), im(lambda i, j: (i, gate_blk + j))),
            pl.BlockSpec((tm, tn), im(lambda i, j: (i, gate_blk + per_gate + j))),
            pl.BlockSpec((tm, tn), im(lambda i, j: (i, gate_blk + 2 * per_gate + j))),
        ],
        out_specs=pl.BlockSpec((tm, tn), im(lambda i, j: (i, j))),
        out_shape=jax.ShapeDtypeStruct((t, d), bf16),
        compiler_params=_params("arbitrary", "arbitrary"),
        name="merge",
    )(ma, gb_, yn, wa, wb, wc, proj, proj, proj)


def _wo_kernel(m_ref, w_ref, x_ref, g_ref, o_ref):
    o_ref[...] = x_ref[...] + g_ref[0] * jnp.dot(m_ref[...], w_ref[...], preferred_element_type=f32)


def _wo(merged, w_o, xf, mod, tm, tn, cols_outer):
    t, d = xf.shape
    tiles_per_batch = SEQ // tm
    grid, im = _tile_grid(t // tm, d // tn, cols_outer)
    return pl.pallas_call(
        _wo_kernel,
        grid=grid,
        in_specs=[
            pl.BlockSpec((tm, d), im(lambda i, j: (i, 0))),
            pl.BlockSpec((d, tn), im(lambda i, j: (0, j))),
            pl.BlockSpec((tm, tn), im(lambda i, j: (i, j))),
            pl.BlockSpec((1, 1, tn), im(lambda i, j: ((i // tiles_per_batch) * N_ADA + G1, 0, j))),
        ],
        out_specs=pl.BlockSpec((tm, tn), im(lambda i, j: (i, j))),
        out_shape=jax.ShapeDtypeStruct((t, d), f32),
        compiler_params=_params("arbitrary", "arbitrary"),
        name="wo",
    )(merged, w_o, xf, mod)


def _mlp_kernel(x_ref, sh_ref, sc_ref, g_ref, wu_ref, wd_ref, fn_ref, o_ref, h_ref, *, final_norm):
    f = pl.program_id(1)

    @pl.when(f == 0)
    def _():
        _mod_rmsnorm(x_ref, sh_ref, sc_ref, h_ref)
        o_ref[...] = jnp.zeros_like(o_ref)

    u = jnp.dot(h_ref[...], wu_ref[...], preferred_element_type=f32)
    u = jnp.square(jnp.maximum(u, 0.0)).astype(bf16)
    o_ref[...] += jnp.dot(u, wd_ref[...], preferred_element_type=f32)

    @pl.when(f == pl.num_programs(1) - 1)
    def _():
        rows = 256
        gate = g_ref[0]

        def body(r, carry):
            rs = pl.ds(pl.multiple_of(r * rows, rows), rows)
            xn = x_ref[rs, :] + gate * o_ref[rs, :]
            if final_norm:
                ms = jnp.mean(xn * xn, axis=-1, keepdims=True)
                xn = (xn * lax.rsqrt(ms + RMS_EPS)) * fn_ref[...]
            o_ref[rs, :] = xn
            return carry

        lax.fori_loop(0, x_ref.shape[0] // rows, body, 0)


def _mlp(xf, mod, w_up, w_down, fn, final_norm, tm=1024, tf=512):
    t, d = xf.shape
    tiles_per_batch = SEQ // tm
    mod_spec = lambda k: pl.BlockSpec((1, 1, d), lambda i, f: ((i // tiles_per_batch) * N_ADA + k, 0, 0))
    return pl.pallas_call(
        functools.partial(_mlp_kernel, final_norm=final_norm),
        grid=(t // tm, D_FF // tf),
        in_specs=[
            pl.BlockSpec((tm, d), lambda i, f: (i, 0)),
            mod_spec(SH2),
            mod_spec(SC2),
            mod_spec(G2),
            pl.BlockSpec((d, tf), lambda i, f: (0, f)),
            pl.BlockSpec((tf, d), lambda i, f: (f, 0)),
            pl.BlockSpec((1, d), lambda i, f: (0, 0)),
        ],
        out_specs=pl.BlockSpec((tm, d), lambda i, f: (i, 0)),
        out_shape=jax.ShapeDtypeStruct((t, d), f32),
        scratch_shapes=[pltpu.VMEM((tm, d), bf16)],
        compiler_params=_params("parallel", "arbitrary"),
        name="mlp",
    )(xf, mod, mod, mod, w_up, w_down, fn)


WIN_TILE = 512
W_ALIGN = 32


def _win_src_row(j):
    c = j * WIN_TILE
    src = jnp.where(c < GATE_OFF, c - XBC_OFF + W_XBC,
                    jnp.where(c < UV_OFF, c - GATE_OFF + W_GATE,
                              jnp.where(c < Z_OFF, c - UV_OFF + W_UV,
                                        jnp.where(c < P_OFF, c - Z_OFF + W_Z, c - P_OFF))))
    return pl.multiple_of(src, W_ALIGN)


def _win_kernel(wt_ref, o_ref):
    o_ref[...] = wt_ref[0].T.astype(o_ref.dtype)


def _win_prep(w_in_t, layer):
    _, n, d = w_in_t.shape
    assert n == W_END and all(o % WIN_TILE == 0 for o in (XBC_OFF, GATE_OFF, UV_OFF, Z_OFF, P_OFF))
    return pl.pallas_call(
        _win_kernel,
        grid=(N_MAIN // WIN_TILE,),
        in_specs=[pl.BlockSpec((pl.Element(1), pl.Element(WIN_TILE), pl.Element(d)),
                               lambda j: (layer, _win_src_row(j), 0))],
        out_specs=pl.BlockSpec((d, WIN_TILE), lambda j: (0, j)),
        out_shape=jax.ShapeDtypeStruct((d, N_MAIN), bf16),
        compiler_params=_params("parallel"),
        name="winprep",
    )(w_in_t)


def _pad_lanes(v):
    return jnp.pad(v.astype(f32), (0, DT_PAD - v.shape[0])).reshape(1, DT_PAD)


def kernel(x, c, w_ada, b_ada, w_in, pool_w, pool_scale, gmlp_ln_g, gmlp_ln_b, gmlp_ws, gmlp_bs, conv_w, conv_b, dt_bias, a_log, d_skip, ssm_norm, w_pool_out, w_gmlp_out, w_ssm_out, w_o, w_up, w_down, final_norm):
    bsz, seq, d = x.shape
    assert (seq, d) == (SEQ, D_MODEL) and w_ada.shape[0] == DEPTH
    xf = x.reshape(bsz * seq, d)
    fn = final_norm.reshape(1, d)
    w_in_t = jnp.swapaxes(w_in, 1, 2)
    for l in range(DEPTH):
        mod = _ada(c, w_ada, b_ada, l, tk=256).reshape(bsz * N_ADA, 1, d)
        w_main = _win_prep(w_in_t, l)
        proj, dtr = _inproj(xf, mod, w_main, w_in_t, l)
        mixed_a = _pool(proj, pool_w[l].astype(bf16), pool_scale[l].reshape(1, POOL_WIDTH))
        gated_b = _gmlp(proj, gmlp_ln_g[l].reshape(1, GMLP_WIDTH), gmlp_ln_b[l].reshape(1, GMLP_WIDTH),
                        gmlp_ws[l], gmlp_bs[l].reshape(GMLP_HEADS, CHUNK, 1))
        yn = _ssd(proj, dtr, conv_w[l], conv_b[l].reshape(1, SSM_CONV_DIM), _pad_lanes(dt_bias[l]),
                  _pad_lanes(a_log[l]), jnp.repeat(d_skip[l], SSM_HEAD_DIM).reshape(1, SSM_INNER),
                  ssm_norm[l].reshape(1, SSM_INNER), bsz)
        tiling = dict(tm=512, tn=1024, cols_outer=True)
        merged = _merge(mixed_a, gated_b, yn, proj, w_pool_out[l].astype(bf16), w_gmlp_out[l].astype(bf16),
                        w_ssm_out[l].astype(bf16), **tiling)
        xf = _wo(merged, w_o[l].astype(bf16), xf, mod, **tiling)
        xf = _mlp(xf, mod, w_up[l].astype(bf16), w_down[l].astype(bf16), fn, final_norm=(l == DEPTH - 1))
    return xf.reshape(bsz, seq, d)
```

```python
import functools

import jax
import jax.numpy as jnp
from jax import lax
from jax.experimental import pallas as pl
from jax.experimental.pallas import tpu as pltpu

f32 = jnp.float32
bf16 = jnp.bfloat16

D_MODEL = 2048
SEQ = 2048
DEPTH = 2
POOL_WIDTH = 1024
POOL_WINDOWS = (2, 4, 8, 16)
POOL_GROUP_DIM = 256
POOL_HALO = 16
GMLP_WIDTH = 1024
GMLP_HEADS = 8
GMLP_HEAD_DIM = 128
CHUNK = 128
SSM_INNER = 2048
SSM_HEAD_DIM = 64
SSM_HEADS = 32
SSM_GROUPS = 8
SSM_STATE = 128
SSM_CONV = 4
CONV_HALO = 8
SSM_CONV_DIM = 4096
GROUP_COLS = SSM_INNER // SSM_GROUPS
N_ADA = 6
D_FF = 8192
RMS_EPS = 1e-6
LN_EPS = 1e-5
LANES = 128
DT_PAD = LANES

XBC_OFF = 0
GATE_OFF = 4096
UV_OFF = 10240
Z_OFF = 12288
P_OFF = 14336
N_MAIN = 15360
W_UV, W_Z, W_XBC, W_DT, W_GATE, W_END = 1024, 3072, 5120, 9216, 9248, 15392

VMEM_LIMIT = 56 * 1024 * 1024

SH1, SC1, G1, SH2, SC2, G2 = range(N_ADA)


def _params(*sem):
    return pltpu.CompilerParams(dimension_semantics=sem, vmem_limit_bytes=VMEM_LIMIT)


def _split_bf16(x, n):
    parts, r = [], x
    for _ in range(n):
        p = r.astype(bf16)
        parts.append(p)
        r = r - p.astype(f32)
    return parts


def _mod_rmsnorm(x_ref, sh_ref, sc_ref, h_ref, rows=256):
    sh = sh_ref[0]
    sc1p = 1.0 + sc_ref[0]

    def body(r, carry):
        rs = pl.ds(pl.multiple_of(r * rows, rows), rows)
        xf = x_ref[rs, :]
        ms = jnp.mean(xf * xf, axis=-1, keepdims=True)
        h_ref[rs, :] = ((xf * lax.rsqrt(ms + RMS_EPS)) * sc1p + sh).astype(h_ref.dtype)
        return carry

    lax.fori_loop(0, x_ref.shape[0] // rows, body, 0)


def _ada_kernel(c_ref, w_ref, b_ref, o_ref):
    c_act = jax.nn.silu(c_ref[...]).astype(bf16)
    part = jnp.dot(c_act, w_ref[0].astype(bf16), preferred_element_type=f32)

    @pl.when(pl.program_id(0) == 0)
    def _():
        o_ref[...] = part + b_ref[0]

    @pl.when(pl.program_id(0) > 0)
    def _():
        o_ref[...] += part


def _ada(c, w_ada, b_ada, layer, tk):
    depth, d, n = w_ada.shape
    bsz = c.shape[0]
    return pl.pallas_call(
        _ada_kernel,
        grid=(d // tk,),
        in_specs=[
            pl.BlockSpec((bsz, tk), lambda k: (0, k)),
            pl.BlockSpec((1, tk, n), lambda k: (layer, k, 0)),
            pl.BlockSpec((1, 1, n), lambda k: (layer, 0, 0)),
        ],
        out_specs=pl.BlockSpec((bsz, n), lambda k: (0, 0)),
        out_shape=jax.ShapeDtypeStruct((bsz, n), f32),
        compiler_params=_params("arbitrary"),
        name="ada",
    )(c, w_ada, b_ada.reshape(depth, 1, n))


def _inproj_kernel(x_ref, sh_ref, sc_ref, w_ref, wdt_ref, o_ref, odt_ref, h_ref):
    @pl.when(pl.program_id(1) == 0)
    def _():
        _mod_rmsnorm(x_ref, sh_ref, sc_ref, h_ref)
        dt_all = lax.dot_general(h_ref[...], wdt_ref[0].astype(bf16), (((1,), (1,)), ((), ())),
                                 preferred_element_type=f32)
        lane = lax.broadcasted_iota(jnp.int32, dt_all.shape, 1)
        odt_ref[...] = jnp.where(lane < SSM_HEADS, dt_all, 0.0)

    o_ref[...] = jnp.dot(h_ref[...], w_ref[...], preferred_element_type=f32)


def _inproj(xf, mod, w_main, w_in_t, layer, tm=1024, tn=1024):
    t, d = xf.shape
    tiles_per_seq = SEQ // tm
    return pl.pallas_call(
        _inproj_kernel,
        grid=(t // tm, N_MAIN // tn),
        in_specs=[
            pl.BlockSpec((tm, d), lambda i, j: (i, 0)),
            pl.BlockSpec((1, 1, d), lambda i, j: ((i // tiles_per_seq) * N_ADA + SH1, 0, 0)),
            pl.BlockSpec((1, 1, d), lambda i, j: ((i // tiles_per_seq) * N_ADA + SC1, 0, 0)),
            pl.BlockSpec((d, tn), lambda i, j: (0, j)),
            pl.BlockSpec((pl.Element(1), pl.Element(DT_PAD), pl.Element(d)), lambda i, j: (layer, W_DT, 0)),
        ],
        out_specs=[
            pl.BlockSpec((tm, tn), lambda i, j: (i, j)),
            pl.BlockSpec((tm, DT_PAD), lambda i, j: (i, 0)),
        ],
        out_shape=[
            jax.ShapeDtypeStruct((t, N_MAIN), f32),
            jax.ShapeDtypeStruct((t, DT_PAD), f32),
        ],
        scratch_shapes=[pltpu.VMEM((tm, d), bf16)],
        compiler_params=_params("parallel", "arbitrary"),
        name="inproj",
    )(xf, mod, mod, w_main, w_in_t)


def _pool_kernel(p_ref, halo_ref, pw_ref, scale_ref, o_ref, *, tiles_per_seq):
    ts = p_ref.shape[0]
    tile_in_seq = pl.program_id(0) % tiles_per_seq
    first = tile_in_seq == 0
    pos = lax.broadcasted_iota(jnp.int32, (ts, 1), 0) + tile_in_seq * ts
    for g, w in enumerate(POOL_WINDOWS):
        cols = slice(g * POOL_GROUP_DIM, (g + 1) * POOL_GROUP_DIM)
        pg = p_ref[:, cols]
        halo = jnp.where(first, 0.0, halo_ref[:, cols])
        s = jnp.concatenate([halo, pg], axis=0)
        k = 1
        while k < w:
            s = s + pltpu.roll(s, k, axis=0)
            k *= 2
        count = jnp.minimum(pos + 1, w).astype(f32)
        pooled = s[POOL_HALO:, :] / count - pg
        mixed = jnp.dot(pooled.astype(bf16), pw_ref[g], preferred_element_type=f32)
        o_ref[:, cols] = (mixed * scale_ref[:, cols]).astype(o_ref.dtype)


def _pool(proj, pool_w, pool_scale, ts=512):
    t = proj.shape[0]
    halo_blocks = ts // POOL_HALO
    p_blk = P_OFF // POOL_WIDTH
    return pl.pallas_call(
        functools.partial(_pool_kernel, tiles_per_seq=SEQ // ts),
        grid=(t // ts,),
        in_specs=[
            pl.BlockSpec((ts, POOL_WIDTH), lambda i: (i, p_blk)),
            pl.BlockSpec((POOL_HALO, POOL_WIDTH), lambda i: (jnp.maximum(i * halo_blocks - 1, 0), p_blk)),
            pl.BlockSpec(pool_w.shape, lambda i: (0, 0, 0)),
            pl.BlockSpec((1, POOL_WIDTH), lambda i: (0, 0)),
        ],
        out_specs=pl.BlockSpec((ts, POOL_WIDTH), lambda i: (i, 0)),
        out_shape=jax.ShapeDtypeStruct((t, POOL_WIDTH), bf16),
        compiler_params=_params("parallel"),
        name="pool",
    )(proj, proj, pool_w, pool_scale)


def _gelu(x):
    return 0.5 * x * (1.0 + lax.erf(x * 0.7071067811865476))


def _gmlp_kernel(u_ref, v_ref, g_ref, b_ref, ws_ref, bs_ref, o_ref, vn_ref):
    tg = u_ref.shape[0]
    v = _gelu(v_ref[...])
    mu = jnp.mean(v, axis=-1, keepdims=True)
    dv = v - mu
    var = jnp.mean(dv * dv, axis=-1, keepdims=True)
    vn_ref[...] = ((dv * lax.rsqrt(var + LN_EPS)) * g_ref[...] + b_ref[...]).astype(vn_ref.dtype)
    ri = lax.broadcasted_iota(jnp.int32, (CHUNK, CHUNK), 0)
    ci = lax.broadcasted_iota(jnp.int32, (CHUNK, CHUNK), 1)
    causal = ri >= ci
    for h in range(GMLP_HEADS):
        cols = slice(h * GMLP_HEAD_DIM, (h + 1) * GMLP_HEAD_DIM)
        wm = jnp.where(causal, ws_ref[h], 0.0).astype(bf16)
        bias = bs_ref[h]
        for c in range(tg // CHUNK):
            rows = slice(c * CHUNK, (c + 1) * CHUNK)
            mixed = jnp.dot(wm, vn_ref[rows, cols], preferred_element_type=f32) + bias
            o_ref[rows, cols] = (_gelu(u_ref[rows, cols]) * mixed).astype(o_ref.dtype)


def _gmlp(proj, ln_g, ln_b, ws, bs, tg=512):
    t = proj.shape[0]
    u_blk = UV_OFF // GMLP_WIDTH
    return pl.pallas_call(
        _gmlp_kernel,
        grid=(t // tg,),
        in_specs=[
            pl.BlockSpec((tg, GMLP_WIDTH), lambda i: (i, u_blk)),
            pl.BlockSpec((tg, GMLP_WIDTH), lambda i: (i, u_blk + 1)),
            pl.BlockSpec((1, GMLP_WIDTH), lambda i: (0, 0)),
            pl.BlockSpec((1, GMLP_WIDTH), lambda i: (0, 0)),
            pl.BlockSpec((GMLP_HEADS, CHUNK, CHUNK), lambda i: (0, 0, 0)),
            pl.BlockSpec((GMLP_HEADS, CHUNK, 1), lambda i: (0, 0, 0)),
        ],
        out_specs=pl.BlockSpec((tg, GMLP_WIDTH), lambda i: (i, 0)),
        out_shape=jax.ShapeDtypeStruct((t, GMLP_WIDTH), bf16),
        scratch_shapes=[pltpu.VMEM((tg, GMLP_WIDTH), bf16)],
        compiler_params=_params("parallel"),
        name="gmlp",
    )(proj, proj, ln_g, ln_b, ws, bs)


def _ssd_chunk(rows, dtr_ref, dtb_ref, alog_ref, z_ref, dskip_ref, ng_ref, o_ref, state_ref, xs_ref,
               bm_ref, cm_ref, wdt_ref, eout_ref, consts):
    causal, tril, expand, low_half = consts
    dt = jax.nn.softplus(dtr_ref[rows, :] + dtb_ref[...])
    da = dt * (-jnp.exp(alog_ref[...]))
    cs3 = jnp.dot(tril, jnp.concatenate(_split_bf16(da, 3), axis=1), preferred_element_type=f32)
    a_cs = cs3[:, :LANES] + cs3[:, LANES:2 * LANES] + cs3[:, 2 * LANES:]
    a_last = a_cs[CHUNK - 1:CHUNK, :]
    wdt = dt * jnp.exp(a_last - a_cs)
    e_out = jnp.exp(a_cs)
    a_cs_t = a_cs.T
    dt_t = dt.T
    pieces = jnp.concatenate(_split_bf16(wdt, 3) + _split_bf16(e_out, 3), axis=0)
    ex = jnp.dot(pieces, expand, preferred_element_type=f32)
    wdt_ref[rows, :] = ex[:CHUNK] + ex[CHUNK:2 * CHUNK] + ex[2 * CHUNK:3 * CHUNK]
    eout_ref[rows, :] = ex[3 * CHUNK:4 * CHUNK] + ex[4 * CHUNK:5 * CHUNK] + ex[5 * CHUNK:]
    last = rows.start + CHUNK - 1

    for g in range(SSM_GROUPS):
        gcols = slice(g * GROUP_COLS, (g + 1) * GROUP_COLS)
        ncols = slice(g * SSM_STATE, (g + 1) * SSM_STATE)
        bg = bm_ref[rows, ncols]
        cg = cm_ref[rows, ncols]
        cb = lax.dot_general(cg, bg, (((1,), (1,)), ((), ())), preferred_element_type=f32)
        y_pairs = []
        for pr in range(2):
            scores = []
            for hh in range(2):
                h = 4 * g + 2 * pr + hh
                diff = a_cs[:, h:h + 1] - a_cs_t[h:h + 1, :]
                decay = jnp.exp(jnp.where(causal, diff, -jnp.inf))
                scores.append((cb * decay * dt_t[h:h + 1, :]).astype(bf16))
            lhs = jnp.concatenate(scores, axis=1)
            xp = xs_ref[rows, g * GROUP_COLS + pr * LANES:g * GROUP_COLS + (pr + 1) * LANES]
            rhs = jnp.concatenate([jnp.where(low_half, xp, 0.0), jnp.where(low_half, 0.0, xp)], axis=0)
            y_pairs.append(jnp.dot(lhs, rhs.astype(bf16), preferred_element_type=f32))
        y = jnp.concatenate(y_pairs, axis=1)
        xs_g = xs_ref[rows, gcols]
        xdw = (xs_g * wdt_ref[rows, gcols]).astype(bf16)
        st_new = lax.dot_general(bg, xdw, (((0,), (0,)), ((), ())), preferred_element_type=f32)
        prev = state_ref[g]
        y = y + jnp.dot(cg, prev.astype(bf16), preferred_element_type=f32) * eout_ref[rows, gcols]
        state_ref[g] = prev * eout_ref[last:last + 1, gcols] + st_new
        y = y + dskip_ref[:, gcols] * xs_g
        y = y * jax.nn.silu(z_ref[rows, gcols])
        y = y * lax.rsqrt(jnp.mean(y * y, axis=-1, keepdims=True) + RMS_EPS)
        o_ref[rows, gcols] = (y * ng_ref[:, gcols]).astype(o_ref.dtype)


def _ssd_kernel(xbc_ref, halo_ref, z_ref, dtr_ref, cw_ref, cb_ref, dtb_ref, alog_ref, dskip_ref,
                ng_ref, o_ref, state_ref, xs_ref, bm_ref, cm_ref, wdt_ref, eout_ref, expand_ref):
    step = pl.program_id(1)

    @pl.when(step == 0)
    def _():
        state_ref[...] = jnp.zeros_like(state_ref)
        hi = lax.broadcasted_iota(jnp.int32, (LANES, SSM_INNER), 0)
        ch = lax.broadcasted_iota(jnp.int32, (LANES, SSM_INNER), 1)
        expand_ref[...] = jnp.where(hi == ch // SSM_HEAD_DIM, 1.0, 0.0).astype(bf16)

    slab, rc = 256, 64
    n_state = SSM_GROUPS * SSM_STATE
    for s in range(SSM_CONV_DIM // slab):
        cols = slice(s * slab, (s + 1) * slab)
        w = cw_ref[:, cols]
        bias = cb_ref[:, cols]
        for r in range(xbc_ref.shape[0] // rc):
            rows = slice(r * rc, (r + 1) * rc)
            if r == 0:
                halo = jnp.where(step == 0, 0.0, halo_ref[:, cols])
            else:
                halo = xbc_ref[r * rc - CONV_HALO:r * rc, cols]
            cur = xbc_ref[rows, cols]
            e = jnp.concatenate([halo, cur], axis=0)
            acc = cur * w[SSM_CONV - 1:SSM_CONV, :]
            for k in range(1, SSM_CONV):
                acc = acc + pltpu.roll(e, k, axis=0)[CONV_HALO:, :] * w[SSM_CONV - 1 - k:SSM_CONV - k, :]
            act = jax.nn.silu(acc + bias)
            if s * slab < SSM_INNER:
                xs_ref[rows, cols] = act
            elif s * slab < SSM_INNER + n_state:
                off = s * slab - SSM_INNER
                bm_ref[rows, off:off + slab] = act.astype(bf16)
            else:
                off = s * slab - SSM_INNER - n_state
                cm_ref[rows, off:off + slab] = act.astype(bf16)

    ri = lax.broadcasted_iota(jnp.int32, (CHUNK, CHUNK), 0)
    ci = lax.broadcasted_iota(jnp.int32, (CHUNK, CHUNK), 1)
    causal = ri >= ci
    tril = jnp.where(causal, 1.0, 0.0).astype(bf16)
    low_half = lax.broadcasted_iota(jnp.int32, (CHUNK, LANES), 1) < SSM_HEAD_DIM
    consts = (causal, tril, expand_ref[...], low_half)
    for cc in range(xbc_ref.shape[0] // CHUNK):
        _ssd_chunk(slice(cc * CHUNK, (cc + 1) * CHUNK), dtr_ref, dtb_ref, alog_ref, z_ref, dskip_ref,
                   ng_ref, o_ref, state_ref, xs_ref, bm_ref, cm_ref, wdt_ref, eout_ref, consts)


def _ssd(proj, dtr, conv_w, conv_b, dt_bias, a_log, d_skip, norm_g, bsz, chunks_per_step=2):
    t = proj.shape[0]
    rb = chunks_per_step * CHUNK
    steps = SEQ // rb
    halo_blocks = rb // CONV_HALO
    row = lambda b, c: b * steps + c
    const = lambda b, c: (0, 0)
    return pl.pallas_call(
        _ssd_kernel,
        grid=(bsz, steps),
        in_specs=[
            pl.BlockSpec((rb, SSM_CONV_DIM), lambda b, c: (row(b, c), XBC_OFF // SSM_CONV_DIM)),
            pl.BlockSpec((CONV_HALO, SSM_CONV_DIM),
                         lambda b, c: (jnp.maximum(row(b, c) * halo_blocks - 1, 0), XBC_OFF // SSM_CONV_DIM)),
            pl.BlockSpec((rb, SSM_INNER), lambda b, c: (row(b, c), Z_OFF // SSM_INNER)),
            pl.BlockSpec((rb, DT_PAD), lambda b, c: (row(b, c), 0)),
            pl.BlockSpec((SSM_CONV, SSM_CONV_DIM), const),
            pl.BlockSpec((1, SSM_CONV_DIM), const),
            pl.BlockSpec((1, DT_PAD), const),
            pl.BlockSpec((1, DT_PAD), const),
            pl.BlockSpec((1, SSM_INNER), const),
            pl.BlockSpec((1, SSM_INNER), const),
        ],
        out_specs=pl.BlockSpec((rb, SSM_INNER), lambda b, c: (row(b, c), 0)),
        out_shape=jax.ShapeDtypeStruct((t, SSM_INNER), bf16),
        scratch_shapes=[
            pltpu.VMEM((SSM_GROUPS, SSM_STATE, GROUP_COLS), f32),
            pltpu.VMEM((rb, SSM_INNER), f32),
            pltpu.VMEM((rb, SSM_GROUPS * SSM_STATE), bf16),
            pltpu.VMEM((rb, SSM_GROUPS * SSM_STATE), bf16),
            pltpu.VMEM((rb, SSM_INNER), f32),
            pltpu.VMEM((rb, SSM_INNER), f32),
            pltpu.VMEM((LANES, SSM_INNER), bf16),
        ],
        compiler_params=_params("parallel", "arbitrary"),
        name="ssd",
    )(proj, proj, proj, dtr, conv_w, conv_b, dt_bias, a_log, d_skip, norm_g)


def _merge_kernel(a_ref, b_ref, c_ref, wa_ref, wb_ref, wc_ref, ga_ref, gb_ref, gc_ref, o_ref):
    ya = jnp.dot(a_ref[...], wa_ref[0], preferred_element_type=f32)
    yb = jnp.dot(b_ref[...], wb_ref[0], preferred_element_type=f32)
    yc = jnp.dot(c_ref[...], wc_ref[0], preferred_element_type=f32)
    merged = (jax.nn.sigmoid(ga_ref[...]) * ya + jax.nn.sigmoid(gb_ref[...]) * yb
              + jax.nn.sigmoid(gc_ref[...]) * yc)
    o_ref[...] = merged.astype(o_ref.dtype)


def _tile_grid(n_row, n_col, cols_outer):
    if cols_outer:
        return (n_col, n_row), lambda f: (lambda j, i: f(i, j))
    return (n_row, n_col), lambda f: f


def _merge(ma, gb_, yn, proj, wa, wb, wc, layer, tm, tn, cols_outer):
    t = proj.shape[0]
    d = D_MODEL
    gate_blk = GATE_OFF // tn
    per_gate = d // tn
    grid, im = _tile_grid(t // tm, d // tn, cols_outer)
    return pl.pallas_call(
        _merge_kernel,
        grid=grid,
        in_specs=[
            pl.BlockSpec((tm, POOL_WIDTH), im(lambda i, j: (i, 0))),
            pl.BlockSpec((tm, GMLP_WIDTH), im(lambda i, j: (i, 0))),
            pl.BlockSpec((tm, SSM_INNER), im(lambda i, j: (i, 0))),
            pl.BlockSpec((1, POOL_WIDTH, tn), im(lambda i, j: (layer, 0, j))),
            pl.BlockSpec((1, GMLP_WIDTH, tn), im(lambda i, j: (layer, 0, j))),
            pl.BlockSpec((1, SSM_INNER, tn), im(lambda i, j: (layer, 0, j))),
            pl.BlockSpec((tm, tn), im(lambda i, j: (i, gate_blk + j))),
            pl.BlockSpec((tm, tn), im(lambda i, j: (i, gate_blk + per_gate + j))),
            pl.BlockSpec((tm, tn), im(lambda i, j: (i, gate_blk + 2 * per_gate + j))),
        ],
        out_specs=pl.BlockSpec((tm, tn), im(lambda i, j: (i, j))),
        out_shape=jax.ShapeDtypeStruct((t, d), bf16),
        compiler_params=_params("arbitrary", "arbitrary"),
        name="merge",
    )(ma, gb_, yn, wa, wb, wc, proj, proj, proj)


def _wo_kernel(m_ref, w_ref, x_ref, g_ref, o_ref):
    o_ref[...] = x_ref[...] + g_ref[0] * jnp.dot(m_ref[...], w_ref[0], preferred_element_type=f32)


def _wo(merged, w_o, xf, mod, layer, tm, tn, cols_outer):
    t, d = xf.shape
    tiles_per_batch = SEQ // tm
    grid, im = _tile_grid(t // tm, d // tn, cols_outer)
    return pl.pallas_call(
        _wo_kernel,
        grid=grid,
        in_specs=[
            pl.BlockSpec((tm, d), im(lambda i, j: (i, 0))),
            pl.BlockSpec((1, d, tn), im(lambda i, j: (layer, 0, j))),
            pl.BlockSpec((tm, tn), im(lambda i, j: (i, j))),
            pl.BlockSpec((1, 1, tn), im(lambda i, j: ((i // tiles_per_batch) * N_ADA + G1, 0, j))),
        ],
        out_specs=pl.BlockSpec((tm, tn), im(lambda i, j: (i, j))),
        out_shape=jax.ShapeDtypeStruct((t, d), f32),
        compiler_params=_params("arbitrary", "arbitrary"),
        name="wo",
    )(merged, w_o, xf, mod)


def _mlp_kernel(x_ref, sh_ref, sc_ref, g_ref, wu_ref, wd_ref, fn_ref, o_ref, h_ref, *, final_norm):
    f = pl.program_id(1)

    @pl.when(f == 0)
    def _():
        _mod_rmsnorm(x_ref, sh_ref, sc_ref, h_ref)
        o_ref[...] = jnp.zeros_like(o_ref)

    u = jnp.dot(h_ref[...], wu_ref[0], preferred_element_type=f32)
    u = jnp.square(jnp.maximum(u, 0.0)).astype(bf16)
    o_ref[...] += jnp.dot(u, wd_ref[0], preferred_element_type=f32)

    @pl.when(f == pl.num_programs(1) - 1)
    def _():
        rows = 256
        gate = g_ref[0]

        def body(r, carry):
            rs = pl.ds(pl.multiple_of(r * rows, rows), rows)
            xn = x_ref[rs, :] + gate * o_ref[rs, :]
            if final_norm:
                ms = jnp.mean(xn * xn, axis=-1, keepdims=True)
                xn = (xn * lax.rsqrt(ms + RMS_EPS)) * fn_ref[...]
            o_ref[rs, :] = xn
            return carry

        lax.fori_loop(0, x_ref.shape[0] // rows, body, 0)


def _mlp(xf, mod, w_up, w_down, layer, fn, final_norm, tm=1024, tf=512):
    t, d = xf.shape
    tiles_per_batch = SEQ // tm
    mod_spec = lambda k: pl.BlockSpec((1, 1, d), lambda i, f: ((i // tiles_per_batch) * N_ADA + k, 0, 0))
    return pl.pallas_call(
        functools.partial(_mlp_kernel, final_norm=final_norm),
        grid=(t // tm, D_FF // tf),
        in_specs=[
            pl.BlockSpec((tm, d), lambda i, f: (i, 0)),
            mod_spec(SH2),
            mod_spec(SC2),
            mod_spec(G2),
            pl.BlockSpec((1, d, tf), lambda i, f: (layer, 0, f)),
            pl.BlockSpec((1, tf, d), lambda i, f: (layer, f, 0)),
            pl.BlockSpec((1, d), lambda i, f: (0, 0)),
        ],
        out_specs=pl.BlockSpec((tm, d), lambda i, f: (i, 0)),
        out_shape=jax.ShapeDtypeStruct((t, d), f32),
        scratch_shapes=[pltpu.VMEM((tm, d), bf16)],
        compiler_params=_params("parallel", "arbitrary"),
        name="mlp",
    )(xf, mod, mod, mod, w_up, w_down, fn)


WIN_TILE = 512
W_ALIGN = 32


def _win_src_row(j):
    c = j * WIN_TILE
    src = jnp.where(c < GATE_OFF, c - XBC_OFF + W_XBC,
                    jnp.where(c < UV_OFF, c - GATE_OFF + W_GATE,
                              jnp.where(c < Z_OFF, c - UV_OFF + W_UV,
                                        jnp.where(c < P_OFF, c - Z_OFF + W_Z, c - P_OFF))))
    return pl.multiple_of(src, W_ALIGN)


def _win_kernel(wt_ref, o_ref):
    o_ref[...] = wt_ref[0].T.astype(o_ref.dtype)


def _win_prep(w_in_t, layer):
    _, n, d = w_in_t.shape
    assert n == W_END and all(o % WIN_TILE == 0 for o in (XBC_OFF, GATE_OFF, UV_OFF, Z_OFF, P_OFF))
    return pl.pallas_call(
        _win_kernel,
        grid=(N_MAIN // WIN_TILE,),
        in_specs=[pl.BlockSpec((pl.Element(1), pl.Element(WIN_TILE), pl.Element(d)),
                               lambda j: (layer, _win_src_row(j), 0))],
        out_specs=pl.BlockSpec((d, WIN_TILE), lambda j: (0, j)),
        out_shape=jax.ShapeDtypeStruct((d, N_MAIN), bf16),
        compiler_params=_params("parallel"),
        name="winprep",
    )(w_in_t)


def _pad_lanes(v):
    return jnp.pad(v.astype(f32), (0, DT_PAD - v.shape[0])).reshape(1, DT_PAD)


def kernel(x, c, w_ada, b_ada, w_in, pool_w, pool_scale, gmlp_ln_g, gmlp_ln_b, gmlp_ws, gmlp_bs, conv_w, conv_b, dt_bias, a_log, d_skip, ssm_norm, w_pool_out, w_gmlp_out, w_ssm_out, w_o, w_up, w_down, final_norm):
    bsz, seq, d = x.shape
    assert (seq, d) == (SEQ, D_MODEL) and w_ada.shape[0] == DEPTH
    xf = x.reshape(bsz * seq, d)
    fn = final_norm.reshape(1, d)
    w_in_t = jnp.swapaxes(w_in, 1, 2)
    w_a, w_b, w_c = w_pool_out.astype(bf16), w_gmlp_out.astype(bf16), w_ssm_out.astype(bf16)
    w_out, w_up_b, w_down_b = w_o.astype(bf16), w_up.astype(bf16), w_down.astype(bf16)
    for l in range(DEPTH):
        mod = _ada(c, w_ada, b_ada, l, tk=256).reshape(bsz * N_ADA, 1, d)
        w_main = _win_prep(w_in_t, l)
        proj, dtr = _inproj(xf, mod, w_main, w_in_t, l, tn=1536 if l == 0 else 1024)
        mixed_a = _pool(proj, pool_w[l].astype(bf16), pool_scale[l].reshape(1, POOL_WIDTH))
        gated_b = _gmlp(proj, gmlp_ln_g[l].reshape(1, GMLP_WIDTH), gmlp_ln_b[l].reshape(1, GMLP_WIDTH),
                        gmlp_ws[l], gmlp_bs[l].reshape(GMLP_HEADS, CHUNK, 1))
        yn = _ssd(proj, dtr, conv_w[l], conv_b[l].reshape(1, SSM_CONV_DIM), _pad_lanes(dt_bias[l]),
                  _pad_lanes(a_log[l]), jnp.repeat(d_skip[l], SSM_HEAD_DIM).reshape(1, SSM_INNER),
                  ssm_norm[l].reshape(1, SSM_INNER), bsz)
        tiling = dict(tm=512, tn=1024, cols_outer=True)
        merged = _merge(mixed_a, gated_b, yn, proj, w_a, w_b, w_c, l, **tiling)
        xf = _wo(merged, w_out, xf, mod, l, **tiling)
        xf = _mlp(xf, mod, w_up_b, w_down_b, l, fn, final_norm=(l == DEPTH - 1))
    return xf.reshape(bsz, seq, d)
```

```python
import functools

import jax
import jax.numpy as jnp
from jax import lax
from jax.experimental import pallas as pl
from jax.experimental.pallas import tpu as pltpu

f32 = jnp.float32
bf16 = jnp.bfloat16

D_MODEL = 2048
SEQ = 2048
DEPTH = 2
POOL_WIDTH = 1024
POOL_WINDOWS = (2, 4, 8, 16)
POOL_GROUP_DIM = 256
POOL_HALO = 16
GMLP_WIDTH = 1024
GMLP_HEADS = 8
GMLP_HEAD_DIM = 128
CHUNK = 128
SSM_INNER = 2048
SSM_HEAD_DIM = 64
SSM_HEADS = 32
SSM_GROUPS = 8
SSM_STATE = 128
SSM_CONV = 4
CONV_HALO = 8
SSM_CONV_DIM = 4096
GROUP_COLS = SSM_INNER // SSM_GROUPS
N_ADA = 6
D_FF = 8192
RMS_EPS = 1e-6
LN_EPS = 1e-5
LANES = 128
DT_PAD = LANES

XBC_OFF = 0
GATE_OFF = 4096
UV_OFF = 10240
Z_OFF = 12288
P_OFF = 14336
N_MAIN = 15360
W_UV, W_Z, W_XBC, W_DT, W_GATE, W_END = 1024, 3072, 5120, 9216, 9248, 15392

VMEM_LIMIT = 56 * 1024 * 1024

SH1, SC1, G1, SH2, SC2, G2 = range(N_ADA)


def _params(*sem):
    return pltpu.CompilerParams(dimension_semantics=sem, vmem_limit_bytes=VMEM_LIMIT)


def _split_bf16(x, n):
    parts, r = [], x
    for _ in range(n):
        p = r.astype(bf16)
        parts.append(p)
        r = r - p.astype(f32)
    return parts


def _mod_rmsnorm(x_ref, sh_ref, sc_ref, h_ref, rows=256):
    sh = sh_ref[0]
    sc1p = 1.0 + sc_ref[0]

    def body(r, carry):
        rs = pl.ds(pl.multiple_of(r * rows, rows), rows)
        xf = x_ref[rs, :]
        ms = jnp.mean(xf * xf, axis=-1, keepdims=True)
        h_ref[rs, :] = ((xf * lax.rsqrt(ms + RMS_EPS)) * sc1p + sh).astype(h_ref.dtype)
        return carry

    lax.fori_loop(0, x_ref.shape[0] // rows, body, 0)


def _ada_kernel(c_ref, w_ref, b_ref, o_ref):
    c_act = jax.nn.silu(c_ref[...]).astype(bf16)
    part = jnp.dot(c_act, w_ref[0].astype(bf16), preferred_element_type=f32)

    @pl.when(pl.program_id(0) == 0)
    def _():
        o_ref[...] = part + b_ref[0]

    @pl.when(pl.program_id(0) > 0)
    def _():
        o_ref[...] += part


def _ada(c, w_ada, b_ada, layer, tk):
    depth, d, n = w_ada.shape
    bsz = c.shape[0]
    return pl.pallas_call(
        _ada_kernel,
        grid=(d // tk,),
        in_specs=[
            pl.BlockSpec((bsz, tk), lambda k: (0, k)),
            pl.BlockSpec((1, tk, n), lambda k: (layer, k, 0)),
            pl.BlockSpec((1, 1, n), lambda k: (layer, 0, 0)),
        ],
        out_specs=pl.BlockSpec((bsz, n), lambda k: (0, 0)),
        out_shape=jax.ShapeDtypeStruct((bsz, n), f32),
        compiler_params=_params("arbitrary"),
        name="ada",
    )(c, w_ada, b_ada.reshape(depth, 1, n))


def _inproj_kernel(x_ref, sh_ref, sc_ref, w_ref, wdt_ref, o_ref, odt_ref, h_ref):
    @pl.when(pl.program_id(1) == 0)
    def _():
        _mod_rmsnorm(x_ref, sh_ref, sc_ref, h_ref)
        dt_all = lax.dot_general(h_ref[...], wdt_ref[0].astype(bf16), (((1,), (1,)), ((), ())),
                                 preferred_element_type=f32)
        lane = lax.broadcasted_iota(jnp.int32, dt_all.shape, 1)
        odt_ref[...] = jnp.where(lane < SSM_HEADS, dt_all, 0.0)

    o_ref[...] = jnp.dot(h_ref[...], w_ref[...], preferred_element_type=f32)


def _inproj(xf, mod, w_main, w_in_t, layer, tm=1024, tn=1536):
    t, d = xf.shape
    tiles_per_seq = SEQ // tm
    return pl.pallas_call(
        _inproj_kernel,
        grid=(t // tm, N_MAIN // tn),
        in_specs=[
            pl.BlockSpec((tm, d), lambda i, j: (i, 0)),
            pl.BlockSpec((1, 1, d), lambda i, j: ((i // tiles_per_seq) * N_ADA + SH1, 0, 0)),
            pl.BlockSpec((1, 1, d), lambda i, j: ((i // tiles_per_seq) * N_ADA + SC1, 0, 0)),
            pl.BlockSpec((d, tn), lambda i, j: (0, j)),
            pl.BlockSpec((pl.Element(1), pl.Element(DT_PAD), pl.Element(d)), lambda i, j: (layer, W_DT, 0)),
        ],
        out_specs=[
            pl.BlockSpec((tm, tn), lambda i, j: (i, j)),
            pl.BlockSpec((tm, DT_PAD), lambda i, j: (i, 0)),
        ],
        out_shape=[
            jax.ShapeDtypeStruct((t, N_MAIN), f32),
            jax.ShapeDtypeStruct((t, DT_PAD), f32),
        ],
        scratch_shapes=[pltpu.VMEM((tm, d), bf16)],
        compiler_params=_params("parallel", "arbitrary"),
        name="inproj",
    )(xf, mod, mod, w_main, w_in_t)


def _pool_kernel(p_ref, halo_ref, pw_ref, scale_ref, o_ref, *, tiles_per_seq):
    ts = p_ref.shape[0]
    tile_in_seq = pl.program_id(0) % tiles_per_seq
    first = tile_in_seq == 0
    pos = lax.broadcasted_iota(jnp.int32, (ts, 1), 0) + tile_in_seq * ts
    for g, w in enumerate(POOL_WINDOWS):
        cols = slice(g * POOL_GROUP_DIM, (g + 1) * POOL_GROUP_DIM)
        pg = p_ref[:, cols]
        halo = jnp.where(first, 0.0, halo_ref[:, cols])
        s = jnp.concatenate([halo, pg], axis=0)
        k = 1
        while k < w:
            s = s + pltpu.roll(s, k, axis=0)
            k *= 2
        count = jnp.minimum(pos + 1, w).astype(f32)
        pooled = s[POOL_HALO:, :] / count - pg
        mixed = jnp.dot(pooled.astype(bf16), pw_ref[g], preferred_element_type=f32)
        o_ref[:, cols] = (mixed * scale_ref[:, cols]).astype(o_ref.dtype)


def _pool(proj, pool_w, pool_scale, ts=512):
    t = proj.shape[0]
    halo_blocks = ts // POOL_HALO
    p_blk = P_OFF // POOL_WIDTH
    return pl.pallas_call(
        functools.partial(_pool_kernel, tiles_per_seq=SEQ // ts),
        grid=(t // ts,),
        in_specs=[
            pl.BlockSpec((ts, POOL_WIDTH), lambda i: (i, p_blk)),
            pl.BlockSpec((POOL_HALO, POOL_WIDTH), lambda i: (jnp.maximum(i * halo_blocks - 1, 0), p_blk)),
            pl.BlockSpec(pool_w.shape, lambda i: (0, 0, 0)),
            pl.BlockSpec((1, POOL_WIDTH), lambda i: (0, 0)),
        ],
        out_specs=pl.BlockSpec((ts, POOL_WIDTH), lambda i: (i, 0)),
        out_shape=jax.ShapeDtypeStruct((t, POOL_WIDTH), bf16),
        compiler_params=_params("parallel"),
        name="pool",
    )(proj, proj, pool_w, pool_scale)


def _gelu(x):
    return 0.5 * x * (1.0 + lax.erf(x * 0.7071067811865476))


def _gmlp_kernel(u_ref, v_ref, g_ref, b_ref, ws_ref, bs_ref, o_ref, vn_ref):
    tg = u_ref.shape[0]
    v = _gelu(v_ref[...])
    mu = jnp.mean(v, axis=-1, keepdims=True)
    dv = v - mu
    var = jnp.mean(dv * dv, axis=-1, keepdims=True)
    vn_ref[...] = ((dv * lax.rsqrt(var + LN_EPS)) * g_ref[...] + b_ref[...]).astype(vn_ref.dtype)
    ri = lax.broadcasted_iota(jnp.int32, (CHUNK, CHUNK), 0)
    ci = lax.broadcasted_iota(jnp.int32, (CHUNK, CHUNK), 1)
    causal = ri >= ci
    for h in range(GMLP_HEADS):
        cols = slice(h * GMLP_HEAD_DIM, (h + 1) * GMLP_HEAD_DIM)
        wm = jnp.where(causal, ws_ref[h], 0.0).astype(bf16)
        bias = bs_ref[h]
        for c in range(tg // CHUNK):
            rows = slice(c * CHUNK, (c + 1) * CHUNK)
            mixed = jnp.dot(wm, vn_ref[rows, cols], preferred_element_type=f32) + bias
            o_ref[rows, cols] = (_gelu(u_ref[rows, cols]) * mixed).astype(o_ref.dtype)


def _gmlp(proj, ln_g, ln_b, ws, bs, tg=512):
    t = proj.shape[0]
    u_blk = UV_OFF // GMLP_WIDTH
    return pl.pallas_call(
        _gmlp_kernel,
        grid=(t // tg,),
        in_specs=[
            pl.BlockSpec((tg, GMLP_WIDTH), lambda i: (i, u_blk)),
            pl.BlockSpec((tg, GMLP_WIDTH), lambda i: (i, u_blk + 1)),
            pl.BlockSpec((1, GMLP_WIDTH), lambda i: (0, 0)),
            pl.BlockSpec((1, GMLP_WIDTH), lambda i: (0, 0)),
            pl.BlockSpec((GMLP_HEADS, CHUNK, CHUNK), lambda i: (0, 0, 0)),
            pl.BlockSpec((GMLP_HEADS, CHUNK, 1), lambda i: (0, 0, 0)),
        ],
        out_specs=pl.BlockSpec((tg, GMLP_WIDTH), lambda i: (i, 0)),
        out_shape=jax.ShapeDtypeStruct((t, GMLP_WIDTH), bf16),
        scratch_shapes=[pltpu.VMEM((tg, GMLP_WIDTH), bf16)],
        compiler_params=_params("parallel"),
        name="gmlp",
    )(proj, proj, ln_g, ln_b, ws, bs)


def _ssd_chunk(rows, dtr_ref, dtb_ref, alog_ref, z_ref, dskip_ref, ng_ref, o_ref, state_ref, xs_ref,
               bm_ref, cm_ref, wdt_ref, eout_ref, consts):
    causal, tril, expand, low_half = consts
    dt = jax.nn.softplus(dtr_ref[rows, :] + dtb_ref[...])
    da = dt * (-jnp.exp(alog_ref[...]))
    cs3 = jnp.dot(tril, jnp.concatenate(_split_bf16(da, 3), axis=1), preferred_element_type=f32)
    a_cs = cs3[:, :LANES] + cs3[:, LANES:2 * LANES] + cs3[:, 2 * LANES:]
    a_last = a_cs[CHUNK - 1:CHUNK, :]
    wdt = dt * jnp.exp(a_last - a_cs)
    e_out = jnp.exp(a_cs)
    a_cs_t = a_cs.T
    dt_t = dt.T
    pieces = jnp.concatenate(_split_bf16(wdt, 3) + _split_bf16(e_out, 3), axis=0)
    ex = jnp.dot(pieces, expand, preferred_element_type=f32)
    wdt_ref[rows, :] = ex[:CHUNK] + ex[CHUNK:2 * CHUNK] + ex[2 * CHUNK:3 * CHUNK]
    eout_ref[rows, :] = ex[3 * CHUNK:4 * CHUNK] + ex[4 * CHUNK:5 * CHUNK] + ex[5 * CHUNK:]
    last = rows.start + CHUNK - 1

    for g in range(SSM_GROUPS):
        gcols = slice(g * GROUP_COLS, (g + 1) * GROUP_COLS)
        ncols = slice(g * SSM_STATE, (g + 1) * SSM_STATE)
        bg = bm_ref[rows, ncols]
        cg = cm_ref[rows, ncols]
        cb = lax.dot_general(cg, bg, (((1,), (1,)), ((), ())), preferred_element_type=f32)
        y_pairs = []
        for pr in range(2):
            scores = []
            for hh in range(2):
                h = 4 * g + 2 * pr + hh
                diff = a_cs[:, h:h + 1] - a_cs_t[h:h + 1, :]
                decay = jnp.exp(jnp.where(causal, diff, -jnp.inf))
                scores.append((cb * decay * dt_t[h:h + 1, :]).astype(bf16))
            lhs = jnp.concatenate(scores, axis=1)
            xp = xs_ref[rows, g * GROUP_COLS + pr * LANES:g * GROUP_COLS + (pr + 1) * LANES]
            rhs = jnp.concatenate([jnp.where(low_half, xp, 0.0), jnp.where(low_half, 0.0, xp)], axis=0)
            y_pairs.append(jnp.dot(lhs, rhs.astype(bf16), preferred_element_type=f32))
        y = jnp.concatenate(y_pairs, axis=1)
        xs_g = xs_ref[rows, gcols]
        xdw = (xs_g * wdt_ref[rows, gcols]).astype(bf16)
        st_new = lax.dot_general(bg, xdw, (((0,), (0,)), ((), ())), preferred_element_type=f32)
        prev = state_ref[g]
        y = y + jnp.dot(cg, prev.astype(bf16), preferred_element_type=f32) * eout_ref[rows, gcols]
        state_ref[g] = prev * eout_ref[last:last + 1, gcols] + st_new
        y = y + dskip_ref[:, gcols] * xs_g
        y = y * jax.nn.silu(z_ref[rows, gcols])
        y = y * lax.rsqrt(jnp.mean(y * y, axis=-1, keepdims=True) + RMS_EPS)
        o_ref[rows, gcols] = (y * ng_ref[:, gcols]).astype(o_ref.dtype)


def _ssd_kernel(xbc_ref, halo_ref, z_ref, dtr_ref, cw_ref, cb_ref, dtb_ref, alog_ref, dskip_ref,
                ng_ref, o_ref, state_ref, xs_ref, bm_ref, cm_ref, wdt_ref, eout_ref, expand_ref):
    step = pl.program_id(1)

    @pl.when(step == 0)
    def _():
        state_ref[...] = jnp.zeros_like(state_ref)
        hi = lax.broadcasted_iota(jnp.int32, (LANES, SSM_INNER), 0)
        ch = lax.broadcasted_iota(jnp.int32, (LANES, SSM_INNER), 1)
        expand_ref[...] = jnp.where(hi == ch // SSM_HEAD_DIM, 1.0, 0.0).astype(bf16)

    slab, rc = 256, 64
    n_state = SSM_GROUPS * SSM_STATE
    for s in range(SSM_CONV_DIM // slab):
        cols = slice(s * slab, (s + 1) * slab)
        w = cw_ref[:, cols]
        bias = cb_ref[:, cols]
        for r in range(xbc_ref.shape[0] // rc):
            rows = slice(r * rc, (r + 1) * rc)
            if r == 0:
                halo = jnp.where(step == 0, 0.0, halo_ref[:, cols])
            else:
                halo = xbc_ref[r * rc - CONV_HALO:r * rc, cols]
            cur = xbc_ref[rows, cols]
            e = jnp.concatenate([halo, cur], axis=0)
            acc = cur * w[SSM_CONV - 1:SSM_CONV, :]
            for k in range(1, SSM_CONV):
                acc = acc + pltpu.roll(e, k, axis=0)[CONV_HALO:, :] * w[SSM_CONV - 1 - k:SSM_CONV - k, :]
            act = jax.nn.silu(acc + bias)
            if s * slab < SSM_INNER:
                xs_ref[rows, cols] = act
            elif s * slab < SSM_INNER + n_state:
                off = s * slab - SSM_INNER
                bm_ref[rows, off:off + slab] = act.astype(bf16)
            else:
                off = s * slab - SSM_INNER - n_state
                cm_ref[rows, off:off + slab] = act.astype(bf16)

    ri = lax.broadcasted_iota(jnp.int32, (CHUNK, CHUNK), 0)
    ci = lax.broadcasted_iota(jnp.int32, (CHUNK, CHUNK), 1)
    causal = ri >= ci
    tril = jnp.where(causal, 1.0, 0.0).astype(bf16)
    low_half = lax.broadcasted_iota(jnp.int32, (CHUNK, LANES), 1) < SSM_HEAD_DIM
    consts = (causal, tril, expand_ref[...], low_half)
    for cc in range(xbc_ref.shape[0] // CHUNK):
        _ssd_chunk(slice(cc * CHUNK, (cc + 1) * CHUNK), dtr_ref, dtb_ref, alog_ref, z_ref, dskip_ref,
                   ng_ref, o_ref, state_ref, xs_ref, bm_ref, cm_ref, wdt_ref, eout_ref, consts)


def _ssd(proj, dtr, conv_w, conv_b, dt_bias, a_log, d_skip, norm_g, bsz, chunks_per_step=2):
    t = proj.shape[0]
    rb = chunks_per_step * CHUNK
    steps = SEQ // rb
    halo_blocks = rb // CONV_HALO
    row = lambda b, c: b * steps + c
    const = lambda b, c: (0, 0)
    return pl.pallas_call(
        _ssd_kernel,
        grid=(bsz, steps),
        in_specs=[
            pl.BlockSpec((rb, SSM_CONV_DIM), lambda b, c: (row(b, c), XBC_OFF // SSM_CONV_DIM)),
            pl.BlockSpec((CONV_HALO, SSM_CONV_DIM),
                         lambda b, c: (jnp.maximum(row(b, c) * halo_blocks - 1, 0), XBC_OFF // SSM_CONV_DIM)),
            pl.BlockSpec((rb, SSM_INNER), lambda b, c: (row(b, c), Z_OFF // SSM_INNER)),
            pl.BlockSpec((rb, DT_PAD), lambda b, c: (row(b, c), 0)),
            pl.BlockSpec((SSM_CONV, SSM_CONV_DIM), const),
            pl.BlockSpec((1, SSM_CONV_DIM), const),
            pl.BlockSpec((1, DT_PAD), const),
            pl.BlockSpec((1, DT_PAD), const),
            pl.BlockSpec((1, SSM_INNER), const),
            pl.BlockSpec((1, SSM_INNER), const),
        ],
        out_specs=pl.BlockSpec((rb, SSM_INNER), lambda b, c: (row(b, c), 0)),
        out_shape=jax.ShapeDtypeStruct((t, SSM_INNER), bf16),
        scratch_shapes=[
            pltpu.VMEM((SSM_GROUPS, SSM_STATE, GROUP_COLS), f32),
            pltpu.VMEM((rb, SSM_INNER), f32),
            pltpu.VMEM((rb, SSM_GROUPS * SSM_STATE), bf16),
            pltpu.VMEM((rb, SSM_GROUPS * SSM_STATE), bf16),
            pltpu.VMEM((rb, SSM_INNER), f32),
            pltpu.VMEM((rb, SSM_INNER), f32),
            pltpu.VMEM((LANES, SSM_INNER), bf16),
        ],
        compiler_params=_params("parallel", "arbitrary"),
        name="ssd",
    )(proj, proj, proj, dtr, conv_w, conv_b, dt_bias, a_log, d_skip, norm_g)


def _merge_kernel(a_ref, b_ref, c_ref, wa_ref, wb_ref, wc_ref, ga_ref, gb_ref, gc_ref, o_ref):
    ya = jnp.dot(a_ref[...], wa_ref[0], preferred_element_type=f32)
    yb = jnp.dot(b_ref[...], wb_ref[0], preferred_element_type=f32)
    yc = jnp.dot(c_ref[...], wc_ref[0], preferred_element_type=f32)
    merged = (jax.nn.sigmoid(ga_ref[...]) * ya + jax.nn.sigmoid(gb_ref[...]) * yb
              + jax.nn.sigmoid(gc_ref[...]) * yc)
    o_ref[...] = merged.astype(o_ref.dtype)


def _tile_grid(n_row, n_col, cols_outer):
    if cols_outer:
        return (n_col, n_row), lambda f: (lambda j, i: f(i, j))
    return (n_row, n_col), lambda f: f


def _merge(ma, gb_, yn, proj, wa, wb, wc, layer, tm, tn, cols_outer):
    t = proj.shape[0]
    d = D_MODEL
    gate_blk = GATE_OFF // tn
    per_gate = d // tn
    grid, im = _tile_grid(t // tm, d // tn, cols_outer)
    return pl.pallas_call(
        _merge_kernel,
        grid=grid,
        in_specs=[
            pl.BlockSpec((tm, POOL_WIDTH), im(lambda i, j: (i, 0))),
            pl.BlockSpec((tm, GMLP_WIDTH), im(lambda i, j: (i, 0))),
            pl.BlockSpec((tm, SSM_INNER), im(lambda i, j: (i, 0))),
            pl.BlockSpec((1, POOL_WIDTH, tn), im(lambda i, j: (layer, 0, j))),
            pl.BlockSpec((1, GMLP_WIDTH, tn), im(lambda i, j: (layer, 0, j))),
            pl.BlockSpec((1, SSM_INNER, tn), im(lambda i, j: (layer, 0, j))),
            pl.BlockSpec((tm, tn), im(lambda i, j: (i, gate_blk + j))),
            pl.BlockSpec((tm, tn), im(lambda i, j: (i, gate_blk + per_gate + j))),
            pl.BlockSpec((tm, tn), im(lambda i, j: (i, gate_blk + 2 * per_gate + j))),
        ],
        out_specs=pl.BlockSpec((tm, tn), im(lambda i, j: (i, j))),
        out_shape=jax.ShapeDtypeStruct((t, d), bf16),
        compiler_params=_params("arbitrary", "arbitrary"),
        name="merge",
    )(ma, gb_, yn, wa, wb, wc, proj, proj, proj)


def _wo_kernel(m_ref, w_ref, x_ref, g_ref, o_ref):
    o_ref[...] = x_ref[...] + g_ref[0] * jnp.dot(m_ref[...], w_ref[0], preferred_element_type=f32)


def _wo(merged, w_o, xf, mod, layer, tm, tn, cols_outer):
    t, d = xf.shape
    tiles_per_batch = SEQ // tm
    grid, im = _tile_grid(t // tm, d // tn, cols_outer)
    return pl.pallas_call(
        _wo_kernel,
        grid=grid,
        in_specs=[
            pl.BlockSpec((tm, d), im(lambda i, j: (i, 0))),
            pl.BlockSpec((1, d, tn), im(lambda i, j: (layer, 0, j))),
            pl.BlockSpec((tm, tn), im(lambda i, j: (i, j))),
            pl.BlockSpec((1, 1, tn), im(lambda i, j: ((i // tiles_per_batch) * N_ADA + G1, 0, j))),
        ],
        out_specs=pl.BlockSpec((tm, tn), im(lambda i, j: (i, j))),
        out_shape=jax.ShapeDtypeStruct((t, d), f32),
        compiler_params=_params("arbitrary", "arbitrary"),
        name="wo",
    )(merged, w_o, xf, mod)


def _mlp_kernel(x_ref, sh_ref, sc_ref, g_ref, wu_ref, wd_ref, fn_ref, o_ref, h_ref, *, final_norm):
    f = pl.program_id(1)

    @pl.when(f == 0)
    def _():
        _mod_rmsnorm(x_ref, sh_ref, sc_ref, h_ref)
        o_ref[...] = jnp.zeros_like(o_ref)

    u = jnp.dot(h_ref[...], wu_ref[0], preferred_element_type=f32)
    u = jnp.square(jnp.maximum(u, 0.0)).astype(bf16)
    o_ref[...] += jnp.dot(u, wd_ref[0], preferred_element_type=f32)

    @pl.when(f == pl.num_programs(1) - 1)
    def _():
        rows = 256
        gate = g_ref[0]

        def body(r, carry):
            rs = pl.ds(pl.multiple_of(r * rows, rows), rows)
            xn = x_ref[rs, :] + gate * o_ref[rs, :]
            if final_norm:
                ms = jnp.mean(xn * xn, axis=-1, keepdims=True)
                xn = (xn * lax.rsqrt(ms + RMS_EPS)) * fn_ref[...]
            o_ref[rs, :] = xn
            return carry

        lax.fori_loop(0, x_ref.shape[0] // rows, body, 0)


def _mlp(xf, mod, w_up, w_down, layer, fn, final_norm, tm=1024, tf=512):
    t, d = xf.shape
    tiles_per_batch = SEQ // tm
    mod_spec = lambda k: pl.BlockSpec((1, 1, d), lambda i, f: ((i // tiles_per_batch) * N_ADA + k, 0, 0))
    return pl.pallas_call(
        functools.partial(_mlp_kernel, final_norm=final_norm),
        grid=(t // tm, D_FF // tf),
        in_specs=[
            pl.BlockSpec((tm, d), lambda i, f: (i, 0)),
            mod_spec(SH2),
            mod_spec(SC2),
            mod_spec(G2),
            pl.BlockSpec((1, d, tf), lambda i, f: (layer, 0, f)),
            pl.BlockSpec((1, tf, d), lambda i, f: (layer, f, 0)),
            pl.BlockSpec((1, d), lambda i, f: (0, 0)),
        ],
        out_specs=pl.BlockSpec((tm, d), lambda i, f: (i, 0)),
        out_shape=jax.ShapeDtypeStruct((t, d), f32),
        scratch_shapes=[pltpu.VMEM((tm, d), bf16)],
        compiler_params=_params("parallel", "arbitrary"),
        name="mlp",
    )(xf, mod, mod, mod, w_up, w_down, fn)


WIN_TILE = 512
W_ALIGN = 32


def _win_src_row(j):
    c = j * WIN_TILE
    src = jnp.where(c < GATE_OFF, c - XBC_OFF + W_XBC,
                    jnp.where(c < UV_OFF, c - GATE_OFF + W_GATE,
                              jnp.where(c < Z_OFF, c - UV_OFF + W_UV,
                                        jnp.where(c < P_OFF, c - Z_OFF + W_Z, c - P_OFF))))
    return pl.multiple_of(src, W_ALIGN)


def _win_kernel(wt_ref, o_ref):
    o_ref[...] = wt_ref[0].T.astype(o_ref.dtype)


def _win_prep(w_in_t, layer):
    _, n, d = w_in_t.shape
    assert n == W_END and all(o % WIN_TILE == 0 for o in (XBC_OFF, GATE_OFF, UV_OFF, Z_OFF, P_OFF))
    return pl.pallas_call(
        _win_kernel,
        grid=(N_MAIN // WIN_TILE,),
        in_specs=[pl.BlockSpec((pl.Element(1), pl.Element(WIN_TILE), pl.Element(d)),
                               lambda j: (layer, _win_src_row(j), 0))],
        out_specs=pl.BlockSpec((d, WIN_TILE), lambda j: (0, j)),
        out_shape=jax.ShapeDtypeStruct((d, N_MAIN), bf16),
        compiler_params=_params("parallel"),
        name="winprep",
    )(w_in_t)


def _pad_lanes(v):
    return jnp.pad(v.astype(f32), (0, DT_PAD - v.shape[0])).reshape(1, DT_PAD)


def kernel(x, c, w_ada, b_ada, w_in, pool_w, pool_scale, gmlp_ln_g, gmlp_ln_b, gmlp_ws, gmlp_bs, conv_w, conv_b, dt_bias, a_log, d_skip, ssm_norm, w_pool_out, w_gmlp_out, w_ssm_out, w_o, w_up, w_down, final_norm):
    bsz, seq, d = x.shape
    assert (seq, d) == (SEQ, D_MODEL) and w_ada.shape[0] == DEPTH
    xf = x.reshape(bsz * seq, d)
    fn = final_norm.reshape(1, d)
    w_in_t = jnp.swapaxes(w_in, 1, 2)
    w_a, w_b, w_c = w_pool_out.astype(bf16), w_gmlp_out.astype(bf16), w_ssm_out.astype(bf16)
    w_out, w_up_b, w_down_b = w_o.astype(bf16), w_up.astype(bf16), w_down.astype(bf16)
    for l in range(DEPTH):
        mod = _ada(c, w_ada, b_ada, l, tk=256).reshape(bsz * N_ADA, 1, d)
        w_main = _win_prep(w_in_t, l)
        proj, dtr = _inproj(xf, mod, w_main, w_in_t, l)
        mixed_a = _pool(proj, pool_w[l].astype(bf16), pool_scale[l].reshape(1, POOL_WIDTH))
        gated_b = _gmlp(proj, gmlp_ln_g[l].reshape(1, GMLP_WIDTH), gmlp_ln_b[l].reshape(1, GMLP_WIDTH),
                        gmlp_ws[l], gmlp_bs[l].reshape(GMLP_HEADS, CHUNK, 1))
        yn = _ssd(proj, dtr, conv_w[l], conv_b[l].reshape(1, SSM_CONV_DIM), _pad_lanes(dt_bias[l]),
                  _pad_lanes(a_log[l]), jnp.repeat(d_skip[l], SSM_HEAD_DIM).reshape(1, SSM_INNER),
                  ssm_norm[l].reshape(1, SSM_INNER), bsz)
        merged = _merge(mixed_a, gated_b, yn, proj, w_a, w_b, w_c, l, tm=512, tn=1024, cols_outer=True)
        xf = _wo(merged, w_out, xf, mod, l, tm=512, tn=2048 if l == 0 else 1024, cols_outer=True)
        xf = _mlp(xf, mod, w_up_b, w_down_b, l, fn, final_norm=(l == DEPTH - 1))
    return xf.reshape(bsz, seq, d)
```

```python
import functools

import jax
import jax.numpy as jnp
from jax import lax
from jax.experimental import pallas as pl
from jax.experimental.pallas import tpu as pltpu

f32 = jnp.float32
bf16 = jnp.bfloat16

D_MODEL = 2048
SEQ = 2048
DEPTH = 2
POOL_WIDTH = 1024
POOL_WINDOWS = (2, 4, 8, 16)
POOL_GROUP_DIM = 256
POOL_HALO = 16
GMLP_WIDTH = 1024
GMLP_HEADS = 8
GMLP_HEAD_DIM = 128
CHUNK = 128
SSM_INNER = 2048
SSM_HEAD_DIM = 64
SSM_HEADS = 32
SSM_GROUPS = 8
SSM_STATE = 128
SSM_CONV = 4
CONV_HALO = 8
SSM_CONV_DIM = 4096
GROUP_COLS = SSM_INNER // SSM_GROUPS
N_ADA = 6
D_FF = 8192
RMS_EPS = 1e-6
LN_EPS = 1e-5
LANES = 128
DT_PAD = LANES

XBC_OFF = 0
GATE_OFF = 4096
UV_OFF = 10240
Z_OFF = 12288
P_OFF = 14336
N_MAIN = 15360
W_UV, W_Z, W_XBC, W_DT, W_GATE, W_END = 1024, 3072, 5120, 9216, 9248, 15392

VMEM_LIMIT = 56 * 1024 * 1024

SH1, SC1, G1, SH2, SC2, G2 = range(N_ADA)


def _params(*sem):
    return pltpu.CompilerParams(dimension_semantics=sem, vmem_limit_bytes=VMEM_LIMIT)


def _split_bf16(x, n):
    parts, r = [], x
    for _ in range(n):
        p = r.astype(bf16)
        parts.append(p)
        r = r - p.astype(f32)
    return parts


def _mod_rmsnorm(x_ref, sh_ref, sc_ref, h_ref, rows=256):
    sh = sh_ref[0]
    sc1p = 1.0 + sc_ref[0]

    def body(r, carry):
        rs = pl.ds(pl.multiple_of(r * rows, rows), rows)
        xf = x_ref[rs, :]
        ms = jnp.mean(xf * xf, axis=-1, keepdims=True)
        h_ref[rs, :] = ((xf * lax.rsqrt(ms + RMS_EPS)) * sc1p + sh).astype(h_ref.dtype)
        return carry

    lax.fori_loop(0, x_ref.shape[0] // rows, body, 0)


def _ada_kernel(c_ref, w_ref, b_ref, o_ref):
    c_act = jax.nn.silu(c_ref[...]).astype(bf16)
    part = jnp.dot(c_act, w_ref[0].astype(bf16), preferred_element_type=f32)

    @pl.when(pl.program_id(0) == 0)
    def _():
        o_ref[...] = part + b_ref[0]

    @pl.when(pl.program_id(0) > 0)
    def _():
        o_ref[...] += part


def _ada(c, w_ada, b_ada, layer, tk):
    depth, d, n = w_ada.shape
    bsz = c.shape[0]
    return pl.pallas_call(
        _ada_kernel,
        grid=(d // tk,),
        in_specs=[
            pl.BlockSpec((bsz, tk), lambda k: (0, k)),
            pl.BlockSpec((1, tk, n), lambda k: (layer, k, 0)),
            pl.BlockSpec((1, 1, n), lambda k: (layer, 0, 0)),
        ],
        out_specs=pl.BlockSpec((bsz, n), lambda k: (0, 0)),
        out_shape=jax.ShapeDtypeStruct((bsz, n), f32),
        compiler_params=_params("arbitrary"),
        name="ada",
    )(c, w_ada, b_ada.reshape(depth, 1, n))


def _inproj_kernel(x_ref, sh_ref, sc_ref, w_ref, wdt_ref, o_ref, odt_ref, h_ref):
    @pl.when(pl.program_id(1) == 0)
    def _():
        _mod_rmsnorm(x_ref, sh_ref, sc_ref, h_ref)
        dt_all = lax.dot_general(h_ref[...], wdt_ref[0].astype(bf16), (((1,), (1,)), ((), ())),
                                 preferred_element_type=f32)
        lane = lax.broadcasted_iota(jnp.int32, dt_all.shape, 1)
        odt_ref[...] = jnp.where(lane < SSM_HEADS, dt_all, 0.0)

    o_ref[...] = jnp.dot(h_ref[...], w_ref[...], preferred_element_type=f32)


def _inproj(xf, mod, w_main, w_in_t, layer, tm=1024, tn=1536):
    t, d = xf.shape
    tiles_per_seq = SEQ // tm
    return pl.pallas_call(
        _inproj_kernel,
        grid=(t // tm, N_MAIN // tn),
        in_specs=[
            pl.BlockSpec((tm, d), lambda i, j: (i, 0)),
            pl.BlockSpec((1, 1, d), lambda i, j: ((i // tiles_per_seq) * N_ADA + SH1, 0, 0)),
            pl.BlockSpec((1, 1, d), lambda i, j: ((i // tiles_per_seq) * N_ADA + SC1, 0, 0)),
            pl.BlockSpec((d, tn), lambda i, j: (0, j)),
            pl.BlockSpec((pl.Element(1), pl.Element(DT_PAD), pl.Element(d)), lambda i, j: (layer, W_DT, 0)),
        ],
        out_specs=[
            pl.BlockSpec((tm, tn), lambda i, j: (i, j)),
            pl.BlockSpec((tm, DT_PAD), lambda i, j: (i, 0)),
        ],
        out_shape=[
            jax.ShapeDtypeStruct((t, N_MAIN), f32),
            jax.ShapeDtypeStruct((t, DT_PAD), f32),
        ],
        scratch_shapes=[pltpu.VMEM((tm, d), bf16)],
        compiler_params=_params("parallel", "arbitrary"),
        name="inproj",
    )(xf, mod, mod, w_main, w_in_t)


def _pool_kernel(p_ref, halo_ref, pw_ref, scale_ref, o_ref, *, tiles_per_seq):
    ts = p_ref.shape[0]
    tile_in_seq = pl.program_id(0) % tiles_per_seq
    first = tile_in_seq == 0
    pos = lax.broadcasted_iota(jnp.int32, (ts, 1), 0) + tile_in_seq * ts
    for g, w in enumerate(POOL_WINDOWS):
        cols = slice(g * POOL_GROUP_DIM, (g + 1) * POOL_GROUP_DIM)
        pg = p_ref[:, cols]
        halo = jnp.where(first, 0.0, halo_ref[:, cols])
        s = jnp.concatenate([halo, pg], axis=0)
        k = 1
        while k < w:
            s = s + pltpu.roll(s, k, axis=0)
            k *= 2
        count = jnp.minimum(pos + 1, w).astype(f32)
        pooled = s[POOL_HALO:, :] / count - pg
        mixed = jnp.dot(pooled.astype(bf16), pw_ref[g], preferred_element_type=f32)
        o_ref[:, cols] = (mixed * scale_ref[:, cols]).astype(o_ref.dtype)


def _pool(proj, pool_w, pool_scale, ts=512):
    t = proj.shape[0]
    halo_blocks = ts // POOL_HALO
    p_blk = P_OFF // POOL_WIDTH
    return pl.pallas_call(
        functools.partial(_pool_kernel, tiles_per_seq=SEQ // ts),
        grid=(t // ts,),
        in_specs=[
            pl.BlockSpec((ts, POOL_WIDTH), lambda i: (i, p_blk)),
            pl.BlockSpec((POOL_HALO, POOL_WIDTH), lambda i: (jnp.maximum(i * halo_blocks - 1, 0), p_blk)),
            pl.BlockSpec(pool_w.shape, lambda i: (0, 0, 0)),
            pl.BlockSpec((1, POOL_WIDTH), lambda i: (0, 0)),
        ],
        out_specs=pl.BlockSpec((ts, POOL_WIDTH), lambda i: (i, 0)),
        out_shape=jax.ShapeDtypeStruct((t, POOL_WIDTH), bf16),
        compiler_params=_params("parallel"),
        name="pool",
    )(proj, proj, pool_w, pool_scale)


def _gelu(x):
    return 0.5 * x * (1.0 + lax.erf(x * 0.7071067811865476))


def _gmlp_kernel(u_ref, v_ref, g_ref, b_ref, ws_ref, bs_ref, o_ref, vn_ref):
    tg = u_ref.shape[0]
    v = _gelu(v_ref[...])
    mu = jnp.mean(v, axis=-1, keepdims=True)
    dv = v - mu
    var = jnp.mean(dv * dv, axis=-1, keepdims=True)
    vn_ref[...] = ((dv * lax.rsqrt(var + LN_EPS)) * g_ref[...] + b_ref[...]).astype(vn_ref.dtype)
    ri = lax.broadcasted_iota(jnp.int32, (CHUNK, CHUNK), 0)
    ci = lax.broadcasted_iota(jnp.int32, (CHUNK, CHUNK), 1)
    causal = ri >= ci
    for h in range(GMLP_HEADS):
        cols = slice(h * GMLP_HEAD_DIM, (h + 1) * GMLP_HEAD_DIM)
        wm = jnp.where(causal, ws_ref[h], 0.0).astype(bf16)
        bias = bs_ref[h]
        for c in range(tg // CHUNK):
            rows = slice(c * CHUNK, (c + 1) * CHUNK)
            mixed = jnp.dot(wm, vn_ref[rows, cols], preferred_element_type=f32) + bias
            o_ref[rows, cols] = (_gelu(u_ref[rows, cols]) * mixed).astype(o_ref.dtype)


def _gmlp(proj, ln_g, ln_b, ws, bs, tg=512):
    t = proj.shape[0]
    u_blk = UV_OFF // GMLP_WIDTH
    return pl.pallas_call(
        _gmlp_kernel,
        grid=(t // tg,),
        in_specs=[
            pl.BlockSpec((tg, GMLP_WIDTH), lambda i: (i, u_blk)),
            pl.BlockSpec((tg, GMLP_WIDTH), lambda i: (i, u_blk + 1)),
            pl.BlockSpec((1, GMLP_WIDTH), lambda i: (0, 0)),
            pl.BlockSpec((1, GMLP_WIDTH), lambda i: (0, 0)),
            pl.BlockSpec((GMLP_HEADS, CHUNK, CHUNK), lambda i: (0, 0, 0)),
            pl.BlockSpec((GMLP_HEADS, CHUNK, 1), lambda i: (0, 0, 0)),
        ],
        out_specs=pl.BlockSpec((tg, GMLP_WIDTH), lambda i: (i, 0)),
        out_shape=jax.ShapeDtypeStruct((t, GMLP_WIDTH), bf16),
        scratch_shapes=[pltpu.VMEM((tg, GMLP_WIDTH), bf16)],
        compiler_params=_params("parallel"),
        name="gmlp",
    )(proj, proj, ln_g, ln_b, ws, bs)


def _ssd_chunk(rows, dtr_ref, dtb_ref, alog_ref, z_ref, dskip_ref, ng_ref, o_ref, state_ref, xs_ref,
               bm_ref, cm_ref, wdt_ref, eout_ref, consts):
    causal, tril, expand, low_half = consts
    dt = jax.nn.softplus(dtr_ref[rows, :] + dtb_ref[...])
    da = dt * (-jnp.exp(alog_ref[...]))
    cs3 = jnp.dot(tril, jnp.concatenate(_split_bf16(da, 3), axis=1), preferred_element_type=f32)
    a_cs = cs3[:, :LANES] + cs3[:, LANES:2 * LANES] + cs3[:, 2 * LANES:]
    a_last = a_cs[CHUNK - 1:CHUNK, :]
    wdt = dt * jnp.exp(a_last - a_cs)
    e_out = jnp.exp(a_cs)
    a_cs_t = a_cs.T
    dt_t = dt.T
    pieces = jnp.concatenate(_split_bf16(wdt, 3) + _split_bf16(e_out, 3), axis=0)
    ex = jnp.dot(pieces, expand, preferred_element_type=f32)
    wdt_ref[rows, :] = ex[:CHUNK] + ex[CHUNK:2 * CHUNK] + ex[2 * CHUNK:3 * CHUNK]
    eout_ref[rows, :] = ex[3 * CHUNK:4 * CHUNK] + ex[4 * CHUNK:5 * CHUNK] + ex[5 * CHUNK:]
    last = rows.start + CHUNK - 1

    for g in range(SSM_GROUPS):
        gcols = slice(g * GROUP_COLS, (g + 1) * GROUP_COLS)
        ncols = slice(g * SSM_STATE, (g + 1) * SSM_STATE)
        bg = bm_ref[rows, ncols]
        cg = cm_ref[rows, ncols]
        cb = lax.dot_general(cg, bg, (((1,), (1,)), ((), ())), preferred_element_type=f32)
        y_pairs = []
        for pr in range(2):
            scores = []
            for hh in range(2):
                h = 4 * g + 2 * pr + hh
                diff = a_cs[:, h:h + 1] - a_cs_t[h:h + 1, :]
                decay = jnp.exp(jnp.where(causal, diff, -jnp.inf))
                scores.append((cb * decay * dt_t[h:h + 1, :]).astype(bf16))
            lhs = jnp.concatenate(scores, axis=1)
            xp = xs_ref[rows, g * GROUP_COLS + pr * LANES:g * GROUP_COLS + (pr + 1) * LANES]
            rhs = jnp.concatenate([jnp.where(low_half, xp, 0.0), jnp.where(low_half, 0.0, xp)], axis=0)
            y_pairs.append(jnp.dot(lhs, rhs.astype(bf16), preferred_element_type=f32))
        y = jnp.concatenate(y_pairs, axis=1)
        xs_g = xs_ref[rows, gcols]
        xdw = (xs_g * wdt_ref[rows, gcols]).astype(bf16)
        st_new = lax.dot_general(bg, xdw, (((0,), (0,)), ((), ())), preferred_element_type=f32)
        prev = state_ref[g]
        y = y + jnp.dot(cg, prev.astype(bf16), preferred_element_type=f32) * eout_ref[rows, gcols]
        state_ref[g] = prev * eout_ref[last:last + 1, gcols] + st_new
        y = y + dskip_ref[:, gcols] * xs_g
        y = y * jax.nn.silu(z_ref[rows, gcols])
        y = y * lax.rsqrt(jnp.mean(y * y, axis=-1, keepdims=True) + RMS_EPS)
        o_ref[rows, gcols] = (y * ng_ref[:, gcols]).astype(o_ref.dtype)


def _ssd_kernel(xbc_ref, halo_ref, z_ref, dtr_ref, cw_ref, cb_ref, dtb_ref, alog_ref, dskip_ref,
                ng_ref, o_ref, state_ref, xs_ref, bm_ref, cm_ref, wdt_ref, eout_ref, expand_ref):
    step = pl.program_id(1)

    @pl.when(step == 0)
    def _():
        state_ref[...] = jnp.zeros_like(state_ref)
        hi = lax.broadcasted_iota(jnp.int32, (LANES, SSM_INNER), 0)
        ch = lax.broadcasted_iota(jnp.int32, (LANES, SSM_INNER), 1)
        expand_ref[...] = jnp.where(hi == ch // SSM_HEAD_DIM, 1.0, 0.0).astype(bf16)

    slab, rc = 256, 64
    n_state = SSM_GROUPS * SSM_STATE
    for s in range(SSM_CONV_DIM // slab):
        cols = slice(s * slab, (s + 1) * slab)
        w = cw_ref[:, cols]
        bias = cb_ref[:, cols]
        for r in range(xbc_ref.shape[0] // rc):
            rows = slice(r * rc, (r + 1) * rc)
            if r == 0:
                halo = jnp.where(step == 0, 0.0, halo_ref[:, cols])
            else:
                halo = xbc_ref[r * rc - CONV_HALO:r * rc, cols]
            cur = xbc_ref[rows, cols]
            e = jnp.concatenate([halo, cur], axis=0)
            acc = cur * w[SSM_CONV - 1:SSM_CONV, :]
            for k in range(1, SSM_CONV):
                acc = acc + pltpu.roll(e, k, axis=0)[CONV_HALO:, :] * w[SSM_CONV - 1 - k:SSM_CONV - k, :]
            act = jax.nn.silu(acc + bias)
            if s * slab < SSM_INNER:
                xs_ref[rows, cols] = act
            elif s * slab < SSM_INNER + n_state:
                off = s * slab - SSM_INNER
                bm_ref[rows, off:off + slab] = act.astype(bf16)
            else:
                off = s * slab - SSM_INNER - n_state
                cm_ref[rows, off:off + slab] = act.astype(bf16)

    ri = lax.broadcasted_iota(jnp.int32, (CHUNK, CHUNK), 0)
    ci = lax.broadcasted_iota(jnp.int32, (CHUNK, CHUNK), 1)
    causal = ri >= ci
    tril = jnp.where(causal, 1.0, 0.0).astype(bf16)
    low_half = lax.broadcasted_iota(jnp.int32, (CHUNK, LANES), 1) < SSM_HEAD_DIM
    consts = (causal, tril, expand_ref[...], low_half)
    for cc in range(xbc_ref.shape[0] // CHUNK):
        _ssd_chunk(slice(cc * CHUNK, (cc + 1) * CHUNK), dtr_ref, dtb_ref, alog_ref, z_ref, dskip_ref,
                   ng_ref, o_ref, state_ref, xs_ref, bm_ref, cm_ref, wdt_ref, eout_ref, consts)


def _ssd(proj, dtr, conv_w, conv_b, dt_bias, a_log, d_skip, norm_g, bsz, chunks_per_step=2):
    t = proj.shape[0]
    rb = chunks_per_step * CHUNK
    steps = SEQ // rb
    halo_blocks = rb // CONV_HALO
    row = lambda b, c: b * steps + c
    const = lambda b, c: (0, 0)
    return pl.pallas_call(
        _ssd_kernel,
        grid=(bsz, steps),
        in_specs=[
            pl.BlockSpec((rb, SSM_CONV_DIM), lambda b, c: (row(b, c), XBC_OFF // SSM_CONV_DIM)),
            pl.BlockSpec((CONV_HALO, SSM_CONV_DIM),
                         lambda b, c: (jnp.maximum(row(b, c) * halo_blocks - 1, 0), XBC_OFF // SSM_CONV_DIM)),
            pl.BlockSpec((rb, SSM_INNER), lambda b, c: (row(b, c), Z_OFF // SSM_INNER)),
            pl.BlockSpec((rb, DT_PAD), lambda b, c: (row(b, c), 0)),
            pl.BlockSpec((SSM_CONV, SSM_CONV_DIM), const),
            pl.BlockSpec((1, SSM_CONV_DIM), const),
            pl.BlockSpec((1, DT_PAD), const),
            pl.BlockSpec((1, DT_PAD), const),
            pl.BlockSpec((1, SSM_INNER), const),
            pl.BlockSpec((1, SSM_INNER), const),
        ],
        out_specs=pl.BlockSpec((rb, SSM_INNER), lambda b, c: (row(b, c), 0)),
        out_shape=jax.ShapeDtypeStruct((t, SSM_INNER), bf16),
        scratch_shapes=[
            pltpu.VMEM((SSM_GROUPS, SSM_STATE, GROUP_COLS), f32),
            pltpu.VMEM((rb, SSM_INNER), f32),
            pltpu.VMEM((rb, SSM_GROUPS * SSM_STATE), bf16),
            pltpu.VMEM((rb, SSM_GROUPS * SSM_STATE), bf16),
            pltpu.VMEM((rb, SSM_INNER), f32),
            pltpu.VMEM((rb, SSM_INNER), f32),
            pltpu.VMEM((LANES, SSM_INNER), bf16),
        ],
        compiler_params=_params("parallel", "arbitrary"),
        name="ssd",
    )(proj, proj, proj, dtr, conv_w, conv_b, dt_bias, a_log, d_skip, norm_g)


def _merge_kernel(a_ref, b_ref, c_ref, wa_ref, wb_ref, wc_ref, ga_ref, gb_ref, gc_ref, o_ref):
    ya = jnp.dot(a_ref[...], wa_ref[0], preferred_element_type=f32)
    yb = jnp.dot(b_ref[...], wb_ref[0], preferred_element_type=f32)
    yc = jnp.dot(c_ref[...], wc_ref[0], preferred_element_type=f32)
    merged = (jax.nn.sigmoid(ga_ref[...]) * ya + jax.nn.sigmoid(gb_ref[...]) * yb
              + jax.nn.sigmoid(gc_ref[...]) * yc)
    o_ref[...] = merged.astype(o_ref.dtype)


def _tile_grid(n_row, n_col, cols_outer):
    if cols_outer:
        return (n_col, n_row), lambda f: (lambda j, i: f(i, j))
    return (n_row, n_col), lambda f: f


def _merge(ma, gb_, yn, proj, wa, wb, wc, layer, tm=512, tn=1024, cols_outer=True):
    t = proj.shape[0]
    d = D_MODEL
    gate_blk = GATE_OFF // tn
    per_gate = d // tn
    grid, im = _tile_grid(t // tm, d // tn, cols_outer)
    return pl.pallas_call(
        _merge_kernel,
        grid=grid,
        in_specs=[
            pl.BlockSpec((tm, POOL_WIDTH), im(lambda i, j: (i, 0))),
            pl.BlockSpec((tm, GMLP_WIDTH), im(lambda i, j: (i, 0))),
            pl.BlockSpec((tm, SSM_INNER), im(lambda i, j: (i, 0))),
            pl.BlockSpec((1, POOL_WIDTH, tn), im(lambda i, j: (layer, 0, j))),
            pl.BlockSpec((1, GMLP_WIDTH, tn), im(lambda i, j: (layer, 0, j))),
            pl.BlockSpec((1, SSM_INNER, tn), im(lambda i, j: (layer, 0, j))),
            pl.BlockSpec((tm, tn), im(lambda i, j: (i, gate_blk + j))),
            pl.BlockSpec((tm, tn), im(lambda i, j: (i, gate_blk + per_gate + j))),
            pl.BlockSpec((tm, tn), im(lambda i, j: (i, gate_blk + 2 * per_gate + j))),
        ],
        out_specs=pl.BlockSpec((tm, tn), im(lambda i, j: (i, j))),
        out_shape=jax.ShapeDtypeStruct((t, d), bf16),
        compiler_params=_params("arbitrary", "arbitrary"),
        name="merge",
    )(ma, gb_, yn, wa, wb, wc, proj, proj, proj)


def _wo_kernel(m_ref, w_ref, x_ref, g_ref, o_ref):
    o_ref[...] = x_ref[...] + g_ref[0] * jnp.dot(m_ref[...], w_ref[0], preferred_element_type=f32)


def _wo(merged, w_o, xf, mod, layer, tm=512, tn=2048, cols_outer=True):
    t, d = xf.shape
    tiles_per_batch = SEQ // tm
    grid, im = _tile_grid(t // tm, d // tn, cols_outer)
    return pl.pallas_call(
        _wo_kernel,
        grid=grid,
        in_specs=[
            pl.BlockSpec((tm, d), im(lambda i, j: (i, 0))),
            pl.BlockSpec((1, d, tn), im(lambda i, j: (layer, 0, j))),
            pl.BlockSpec((tm, tn), im(lambda i, j: (i, j))),
            pl.BlockSpec((1, 1, tn), im(lambda i, j: ((i // tiles_per_batch) * N_ADA + G1, 0, j))),
        ],
        out_specs=pl.BlockSpec((tm, tn), im(lambda i, j: (i, j))),
        out_shape=jax.ShapeDtypeStruct((t, d), f32),
        compiler_params=_params("arbitrary", "arbitrary"),
        name="wo",
    )(merged, w_o, xf, mod)


def _mlp_kernel(x_ref, sh_ref, sc_ref, g_ref, wu_ref, wd_ref, fn_ref, o_ref, h_ref, *, final_norm):
    f = pl.program_id(1)

    @pl.when(f == 0)
    def _():
        _mod_rmsnorm(x_ref, sh_ref, sc_ref, h_ref)
        o_ref[...] = jnp.zeros_like(o_ref)

    u = jnp.dot(h_ref[...], wu_ref[0], preferred_element_type=f32)
    u = jnp.square(jnp.maximum(u, 0.0)).astype(bf16)
    o_ref[...] += jnp.dot(u, wd_ref[0], preferred_element_type=f32)

    @pl.when(f == pl.num_programs(1) - 1)
    def _():
        rows = 256
        gate = g_ref[0]

        def body(r, carry):
            rs = pl.ds(pl.multiple_of(r * rows, rows), rows)
            xn = x_ref[rs, :] + gate * o_ref[rs, :]
            if final_norm:
                ms = jnp.mean(xn * xn, axis=-1, keepdims=True)
                xn = (xn * lax.rsqrt(ms + RMS_EPS)) * fn_ref[...]
            o_ref[rs, :] = xn
            return carry

        lax.fori_loop(0, x_ref.shape[0] // rows, body, 0)


def _mlp(xf, mod, w_up, w_down, layer, fn, final_norm, tm=1024, tf=512):
    t, d = xf.shape
    tiles_per_batch = SEQ // tm
    mod_spec = lambda k: pl.BlockSpec((1, 1, d), lambda i, f: ((i // tiles_per_batch) * N_ADA + k, 0, 0))
    return pl.pallas_call(
        functools.partial(_mlp_kernel, final_norm=final_norm),
        grid=(t // tm, D_FF // tf),
        in_specs=[
            pl.BlockSpec((tm, d), lambda i, f: (i, 0)),
            mod_spec(SH2),
            mod_spec(SC2),
            mod_spec(G2),
            pl.BlockSpec((1, d, tf), lambda i, f: (layer, 0, f)),
            pl.BlockSpec((1, tf, d), lambda i, f: (layer, f, 0)),
            pl.BlockSpec((1, d), lambda i, f: (0, 0)),
        ],
        out_specs=pl.BlockSpec((tm, d), lambda i, f: (i, 0)),
        out_shape=jax.ShapeDtypeStruct((t, d), f32),
        scratch_shapes=[pltpu.VMEM((tm, d), bf16)],
        compiler_params=_params("parallel", "arbitrary"),
        name="mlp",
    )(xf, mod, mod, mod, w_up, w_down, fn)


WIN_TILE = 512
W_ALIGN = 32


def _win_src_row(j):
    c = j * WIN_TILE
    src = jnp.where(c < GATE_OFF, c - XBC_OFF + W_XBC,
                    jnp.where(c < UV_OFF, c - GATE_OFF + W_GATE,
                              jnp.where(c < Z_OFF, c - UV_OFF + W_UV,
                                        jnp.where(c < P_OFF, c - Z_OFF + W_Z, c - P_OFF))))
    return pl.multiple_of(src, W_ALIGN)


def _win_kernel(wt_ref, o_ref):
    o_ref[...] = wt_ref[0].T.astype(o_ref.dtype)


def _win_prep(w_in_t, layer):
    _, n, d = w_in_t.shape
    assert n == W_END and all(o % WIN_TILE == 0 for o in (XBC_OFF, GATE_OFF, UV_OFF, Z_OFF, P_OFF))
    return pl.pallas_call(
        _win_kernel,
        grid=(N_MAIN // WIN_TILE,),
        in_specs=[pl.BlockSpec((pl.Element(1), pl.Element(WIN_TILE), pl.Element(d)),
                               lambda j: (layer, _win_src_row(j), 0))],
        out_specs=pl.BlockSpec((d, WIN_TILE), lambda j: (0, j)),
        out_shape=jax.ShapeDtypeStruct((d, N_MAIN), bf16),
        compiler_params=_params("parallel"),
        name="winprep",
    )(w_in_t)


def _pad_lanes(v):
    return jnp.pad(v.astype(f32), (0, DT_PAD - v.shape[0])).reshape(1, DT_PAD)


def kernel(x, c, w_ada, b_ada, w_in, pool_w, pool_scale, gmlp_ln_g, gmlp_ln_b, gmlp_ws, gmlp_bs, conv_w, conv_b, dt_bias, a_log, d_skip, ssm_norm, w_pool_out, w_gmlp_out, w_ssm_out, w_o, w_up, w_down, final_norm):
    bsz, seq, d = x.shape
    assert (seq, d) == (SEQ, D_MODEL) and w_ada.shape[0] == DEPTH
    xf = x.reshape(bsz * seq, d)
    fn = final_norm.reshape(1, d)
    w_in_t = jnp.swapaxes(w_in, 1, 2)
    w_a, w_b, w_c = w_pool_out.astype(bf16), w_gmlp_out.astype(bf16), w_ssm_out.astype(bf16)
    w_out, w_up_b, w_down_b = w_o.astype(bf16), w_up.astype(bf16), w_down.astype(bf16)
    for l in range(DEPTH):
        mod = _ada(c, w_ada, b_ada, l, tk=256).reshape(bsz * N_ADA, 1, d)
        w_main = _win_prep(w_in_t, l)
        proj, dtr = _inproj(xf, mod, w_main, w_in_t, l)
        rows = 1024 if l == 0 else 512
        mixed_a = _pool(proj, pool_w[l].astype(bf16), pool_scale[l].reshape(1, POOL_WIDTH), ts=rows)
        gated_b = _gmlp(proj, gmlp_ln_g[l].reshape(1, GMLP_WIDTH), gmlp_ln_b[l].reshape(1, GMLP_WIDTH),
                        gmlp_ws[l], gmlp_bs[l].reshape(GMLP_HEADS, CHUNK, 1), tg=rows)
        yn = _ssd(proj, dtr, conv_w[l], conv_b[l].reshape(1, SSM_CONV_DIM), _pad_lanes(dt_bias[l]),
                  _pad_lanes(a_log[l]), jnp.repeat(d_skip[l], SSM_HEAD_DIM).reshape(1, SSM_INNER),
                  ssm_norm[l].reshape(1, SSM_INNER), bsz)
        merged = _merge(mixed_a, gated_b, yn, proj, w_a, w_b, w_c, l)
        xf = _wo(merged, w_out, xf, mod, l)
        xf = _mlp(xf, mod, w_up_b, w_down_b, l, fn, final_norm=(l == DEPTH - 1))
    return xf.reshape(bsz, seq, d)
```

```python
import functools

import jax
import jax.numpy as jnp
from jax import lax
from jax.experimental import pallas as pl
from jax.experimental.pallas import tpu as pltpu

f32 = jnp.float32
bf16 = jnp.bfloat16

D_MODEL = 2048
SEQ = 2048
DEPTH = 2
POOL_WIDTH = 1024
POOL_WINDOWS = (2, 4, 8, 16)
POOL_GROUP_DIM = 256
POOL_HALO = 16
GMLP_WIDTH = 1024
GMLP_HEADS = 8
GMLP_HEAD_DIM = 128
CHUNK = 128
SSM_INNER = 2048
SSM_HEAD_DIM = 64
SSM_HEADS = 32
SSM_GROUPS = 8
SSM_STATE = 128
SSM_CONV = 4
CONV_HALO = 8
SSM_CONV_DIM = 4096
GROUP_COLS = SSM_INNER // SSM_GROUPS
N_ADA = 6
D_FF = 8192
RMS_EPS = 1e-6
LN_EPS = 1e-5
LANES = 128
DT_PAD = LANES

XBC_OFF = 0
GATE_OFF = 4096
UV_OFF = 10240
Z_OFF = 12288
P_OFF = 14336
N_MAIN = 15360
W_UV, W_Z, W_XBC, W_DT, W_GATE, W_END = 1024, 3072, 5120, 9216, 9248, 15392

VMEM_LIMIT = 56 * 1024 * 1024

SH1, SC1, G1, SH2, SC2, G2 = range(N_ADA)


def _params(*sem):
    return pltpu.CompilerParams(dimension_semantics=sem, vmem_limit_bytes=VMEM_LIMIT)


def _split_bf16(x, n):
    parts, r = [], x
    for _ in range(n):
        p = r.astype(bf16)
        parts.append(p)
        r = r - p.astype(f32)
    return parts


def _mod_rmsnorm(x_ref, sh_ref, sc_ref, h_ref, rows=256):
    sh = sh_ref[0]
    sc1p = 1.0 + sc_ref[0]

    def body(r, carry):
        rs = pl.ds(pl.multiple_of(r * rows, rows), rows)
        xf = x_ref[rs, :]
        ms = jnp.mean(xf * xf, axis=-1, keepdims=True)
        h_ref[rs, :] = ((xf * lax.rsqrt(ms + RMS_EPS)) * sc1p + sh).astype(h_ref.dtype)
        return carry

    lax.fori_loop(0, x_ref.shape[0] // rows, body, 0)


def _ada_kernel(c_ref, w_ref, b_ref, o_ref):
    c_act = jax.nn.silu(c_ref[...]).astype(bf16)
    part = jnp.dot(c_act, w_ref[0].astype(bf16), preferred_element_type=f32)

    @pl.when(pl.program_id(0) == 0)
    def _():
        o_ref[...] = part + b_ref[0]

    @pl.when(pl.program_id(0) > 0)
    def _():
        o_ref[...] += part


def _ada(c, w_ada, b_ada, layer, tk):
    depth, d, n = w_ada.shape
    bsz = c.shape[0]
    return pl.pallas_call(
        _ada_kernel,
        grid=(d // tk,),
        in_specs=[
            pl.BlockSpec((bsz, tk), lambda k: (0, k)),
            pl.BlockSpec((1, tk, n), lambda k: (layer, k, 0)),
            pl.BlockSpec((1, 1, n), lambda k: (layer, 0, 0)),
        ],
        out_specs=pl.BlockSpec((bsz, n), lambda k: (0, 0)),
        out_shape=jax.ShapeDtypeStruct((bsz, n), f32),
        compiler_params=_params("arbitrary"),
        name="ada",
    )(c, w_ada, b_ada.reshape(depth, 1, n))


def _inproj_kernel(x_ref, sh_ref, sc_ref, w_ref, wdt_ref, o_ref, odt_ref, h_ref):
    @pl.when(pl.program_id(1) == 0)
    def _():
        _mod_rmsnorm(x_ref, sh_ref, sc_ref, h_ref)
        dt_all = lax.dot_general(h_ref[...], wdt_ref[0].astype(bf16), (((1,), (1,)), ((), ())),
                                 preferred_element_type=f32)
        lane = lax.broadcasted_iota(jnp.int32, dt_all.shape, 1)
        odt_ref[...] = jnp.where(lane < SSM_HEADS, dt_all, 0.0)

    o_ref[...] = jnp.dot(h_ref[...], w_ref[...], preferred_element_type=f32)


def _inproj(xf, mod, w_main, w_in_t, layer, tm=1024, tn=1536):
    t, d = xf.shape
    tiles_per_seq = SEQ // tm
    return pl.pallas_call(
        _inproj_kernel,
        grid=(t // tm, N_MAIN // tn),
        in_specs=[
            pl.BlockSpec((tm, d), lambda i, j: (i, 0)),
            pl.BlockSpec((1, 1, d), lambda i, j: ((i // tiles_per_seq) * N_ADA + SH1, 0, 0)),
            pl.BlockSpec((1, 1, d), lambda i, j: ((i // tiles_per_seq) * N_ADA + SC1, 0, 0)),
            pl.BlockSpec((d, tn), lambda i, j: (0, j)),
            pl.BlockSpec((pl.Element(1), pl.Element(DT_PAD), pl.Element(d)), lambda i, j: (layer, W_DT, 0)),
        ],
        out_specs=[
            pl.BlockSpec((tm, tn), lambda i, j: (i, j)),
            pl.BlockSpec((tm, DT_PAD), lambda i, j: (i, 0)),
        ],
        out_shape=[
            jax.ShapeDtypeStruct((t, N_MAIN), f32),
            jax.ShapeDtypeStruct((t, DT_PAD), f32),
        ],
        scratch_shapes=[pltpu.VMEM((tm, d), bf16)],
        compiler_params=_params("parallel", "arbitrary"),
        name="inproj",
    )(xf, mod, mod, w_main, w_in_t)


def _pool_kernel(p_ref, halo_ref, pw_ref, scale_ref, o_ref, *, tiles_per_seq):
    ts = p_ref.shape[0]
    tile_in_seq = pl.program_id(0) % tiles_per_seq
    first = tile_in_seq == 0
    pos = lax.broadcasted_iota(jnp.int32, (ts, 1), 0) + tile_in_seq * ts
    for g, w in enumerate(POOL_WINDOWS):
        cols = slice(g * POOL_GROUP_DIM, (g + 1) * POOL_GROUP_DIM)
        pg = p_ref[:, cols]
        halo = jnp.where(first, 0.0, halo_ref[:, cols])
        s = jnp.concatenate([halo, pg], axis=0)
        k = 1
        while k < w:
            s = s + pltpu.roll(s, k, axis=0)
            k *= 2
        count = jnp.minimum(pos + 1, w).astype(f32)
        pooled = s[POOL_HALO:, :] / count - pg
        mixed = jnp.dot(pooled.astype(bf16), pw_ref[g], preferred_element_type=f32)
        o_ref[:, cols] = (mixed * scale_ref[:, cols]).astype(o_ref.dtype)


def _pool(proj, pool_w, pool_scale, ts=512):
    t = proj.shape[0]
    halo_blocks = ts // POOL_HALO
    p_blk = P_OFF // POOL_WIDTH
    return pl.pallas_call(
        functools.partial(_pool_kernel, tiles_per_seq=SEQ // ts),
        grid=(t // ts,),
        in_specs=[
            pl.BlockSpec((ts, POOL_WIDTH), lambda i: (i, p_blk)),
            pl.BlockSpec((POOL_HALO, POOL_WIDTH), lambda i: (jnp.maximum(i * halo_blocks - 1, 0), p_blk)),
            pl.BlockSpec(pool_w.shape, lambda i: (0, 0, 0)),
            pl.BlockSpec((1, POOL_WIDTH), lambda i: (0, 0)),
        ],
        out_specs=pl.BlockSpec((ts, POOL_WIDTH), lambda i: (i, 0)),
        out_shape=jax.ShapeDtypeStruct((t, POOL_WIDTH), bf16),
        compiler_params=_params("parallel"),
        name="pool",
    )(proj, proj, pool_w, pool_scale)


def _gelu(x):
    return 0.5 * x * (1.0 + lax.erf(x * 0.7071067811865476))


def _gmlp_kernel(u_ref, v_ref, g_ref, b_ref, ws_ref, bs_ref, o_ref, vn_ref):
    tg = u_ref.shape[0]
    v = _gelu(v_ref[...])
    mu = jnp.mean(v, axis=-1, keepdims=True)
    dv = v - mu
    var = jnp.mean(dv * dv, axis=-1, keepdims=True)
    vn_ref[...] = ((dv * lax.rsqrt(var + LN_EPS)) * g_ref[...] + b_ref[...]).astype(vn_ref.dtype)
    ri = lax.broadcasted_iota(jnp.int32, (CHUNK, CHUNK), 0)
    ci = lax.broadcasted_iota(jnp.int32, (CHUNK, CHUNK), 1)
    causal = ri >= ci
    for h in range(GMLP_HEADS):
        cols = slice(h * GMLP_HEAD_DIM, (h + 1) * GMLP_HEAD_DIM)
        wm = jnp.where(causal, ws_ref[h], 0.0).astype(bf16)
        bias = bs_ref[h]
        for c in range(tg // CHUNK):
            rows = slice(c * CHUNK, (c + 1) * CHUNK)
            mixed = jnp.dot(wm, vn_ref[rows, cols], preferred_element_type=f32) + bias
            o_ref[rows, cols] = (_gelu(u_ref[rows, cols]) * mixed).astype(o_ref.dtype)


def _gmlp(proj, ln_g, ln_b, ws, bs, tg=512):
    t = proj.shape[0]
    u_blk = UV_OFF // GMLP_WIDTH
    return pl.pallas_call(
        _gmlp_kernel,
        grid=(t // tg,),
        in_specs=[
            pl.BlockSpec((tg, GMLP_WIDTH), lambda i: (i, u_blk)),
            pl.BlockSpec((tg, GMLP_WIDTH), lambda i: (i, u_blk + 1)),
            pl.BlockSpec((1, GMLP_WIDTH), lambda i: (0, 0)),
            pl.BlockSpec((1, GMLP_WIDTH), lambda i: (0, 0)),
            pl.BlockSpec((GMLP_HEADS, CHUNK, CHUNK), lambda i: (0, 0, 0)),
            pl.BlockSpec((GMLP_HEADS, CHUNK, 1), lambda i: (0, 0, 0)),
        ],
        out_specs=pl.BlockSpec((tg, GMLP_WIDTH), lambda i: (i, 0)),
        out_shape=jax.ShapeDtypeStruct((t, GMLP_WIDTH), bf16),
        scratch_shapes=[pltpu.VMEM((tg, GMLP_WIDTH), bf16)],
        compiler_params=_params("parallel"),
        name="gmlp",
    )(proj, proj, ln_g, ln_b, ws, bs)


def _ssd_chunk(rows, dtr_ref, dtb_ref, alog_ref, z_ref, dskip_ref, ng_ref, o_ref, state_ref, xs_ref,
               bm_ref, cm_ref, wdt_ref, eout_ref, consts):
    causal, tril, expand, low_half = consts
    dt = jax.nn.softplus(dtr_ref[rows, :] + dtb_ref[...])
    da = dt * (-jnp.exp(alog_ref[...]))
    cs3 = jnp.dot(tril, jnp.concatenate(_split_bf16(da, 3), axis=1), preferred_element_type=f32)
    a_cs = cs3[:, :LANES] + cs3[:, LANES:2 * LANES] + cs3[:, 2 * LANES:]
    a_last = a_cs[CHUNK - 1:CHUNK, :]
    wdt = dt * jnp.exp(a_last - a_cs)
    e_out = jnp.exp(a_cs)
    a_cs_t = a_cs.T
    dt_t = dt.T
    pieces = jnp.concatenate(_split_bf16(wdt, 3) + _split_bf16(e_out, 3), axis=0)
    ex = jnp.dot(pieces, expand, preferred_element_type=f32)
    wdt_ref[rows, :] = ex[:CHUNK] + ex[CHUNK:2 * CHUNK] + ex[2 * CHUNK:3 * CHUNK]
    eout_ref[rows, :] = ex[3 * CHUNK:4 * CHUNK] + ex[4 * CHUNK:5 * CHUNK] + ex[5 * CHUNK:]
    last = rows.start + CHUNK - 1

    for g in range(SSM_GROUPS):
        gcols = slice(g * GROUP_COLS, (g + 1) * GROUP_COLS)
        ncols = slice(g * SSM_STATE, (g + 1) * SSM_STATE)
        bg = bm_ref[rows, ncols]
        cg = cm_ref[rows, ncols]
        cb = lax.dot_general(cg, bg, (((1,), (1,)), ((), ())), preferred_element_type=f32)
        y_pairs = []
        for pr in range(2):
            scores = []
            for hh in range(2):
                h = 4 * g + 2 * pr + hh
                diff = a_cs[:, h:h + 1] - a_cs_t[h:h + 1, :]
                decay = jnp.exp(jnp.where(causal, diff, -jnp.inf))
                scores.append((cb * decay * dt_t[h:h + 1, :]).astype(bf16))
            lhs = jnp.concatenate(scores, axis=1)
            xp = xs_ref[rows, g * GROUP_COLS + pr * LANES:g * GROUP_COLS + (pr + 1) * LANES]
            rhs = jnp.concatenate([jnp.where(low_half, xp, 0.0), jnp.where(low_half, 0.0, xp)], axis=0)
            y_pairs.append(jnp.dot(lhs, rhs.astype(bf16), preferred_element_type=f32))
        y = jnp.concatenate(y_pairs, axis=1)
        xs_g = xs_ref[rows, gcols]
        xdw = (xs_g * wdt_ref[rows, gcols]).astype(bf16)
        st_new = lax.dot_general(bg, xdw, (((0,), (0,)), ((), ())), preferred_element_type=f32)
        prev = state_ref[g]
        y = y + jnp.dot(cg, prev.astype(bf16), preferred_element_type=f32) * eout_ref[rows, gcols]
        state_ref[g] = prev * eout_ref[last:last + 1, gcols] + st_new
        y = y + dskip_ref[:, gcols] * xs_g
        y = y * jax.nn.silu(z_ref[rows, gcols])
        y = y * lax.rsqrt(jnp.mean(y * y, axis=-1, keepdims=True) + RMS_EPS)
        o_ref[rows, gcols] = (y * ng_ref[:, gcols]).astype(o_ref.dtype)


def _ssd_kernel(xbc_ref, halo_ref, z_ref, dtr_ref, cw_ref, cb_ref, dtb_ref, alog_ref, dskip_ref,
                ng_ref, o_ref, state_ref, xs_ref, bm_ref, cm_ref, wdt_ref, eout_ref, expand_ref):
    step = pl.program_id(1)

    @pl.when(step == 0)
    def _():
        state_ref[...] = jnp.zeros_like(state_ref)
        hi = lax.broadcasted_iota(jnp.int32, (LANES, SSM_INNER), 0)
        ch = lax.broadcasted_iota(jnp.int32, (LANES, SSM_INNER), 1)
        expand_ref[...] = jnp.where(hi == ch // SSM_HEAD_DIM, 1.0, 0.0).astype(bf16)

    slab, rc = 256, 64
    n_state = SSM_GROUPS * SSM_STATE
    for s in range(SSM_CONV_DIM // slab):
        cols = slice(s * slab, (s + 1) * slab)
        w = cw_ref[:, cols]
        bias = cb_ref[:, cols]
        for r in range(xbc_ref.shape[0] // rc):
            rows = slice(r * rc, (r + 1) * rc)
            if r == 0:
                halo = jnp.where(step == 0, 0.0, halo_ref[:, cols])
            else:
                halo = xbc_ref[r * rc - CONV_HALO:r * rc, cols]
            cur = xbc_ref[rows, cols]
            e = jnp.concatenate([halo, cur], axis=0)
            acc = cur * w[SSM_CONV - 1:SSM_CONV, :]
            for k in range(1, SSM_CONV):
                acc = acc + pltpu.roll(e, k, axis=0)[CONV_HALO:, :] * w[SSM_CONV - 1 - k:SSM_CONV - k, :]
            act = jax.nn.silu(acc + bias)
            if s * slab < SSM_INNER:
                xs_ref[rows, cols] = act
            elif s * slab < SSM_INNER + n_state:
                off = s * slab - SSM_INNER
                bm_ref[rows, off:off + slab] = act.astype(bf16)
            else:
                off = s * slab - SSM_INNER - n_state
                cm_ref[rows, off:off + slab] = act.astype(bf16)

    ri = lax.broadcasted_iota(jnp.int32, (CHUNK, CHUNK), 0)
    ci = lax.broadcasted_iota(jnp.int32, (CHUNK, CHUNK), 1)
    causal = ri >= ci
    tril = jnp.where(causal, 1.0, 0.0).astype(bf16)
    low_half = lax.broadcasted_iota(jnp.int32, (CHUNK, LANES), 1) < SSM_HEAD_DIM
    consts = (causal, tril, expand_ref[...], low_half)
    for cc in range(xbc_ref.shape[0] // CHUNK):
        _ssd_chunk(slice(cc * CHUNK, (cc + 1) * CHUNK), dtr_ref, dtb_ref, alog_ref, z_ref, dskip_ref,
                   ng_ref, o_ref, state_ref, xs_ref, bm_ref, cm_ref, wdt_ref, eout_ref, consts)


def _ssd(proj, dtr, conv_w, conv_b, dt_bias, a_log, d_skip, norm_g, bsz, chunks_per_step=2):
    t = proj.shape[0]
    rb = chunks_per_step * CHUNK
    steps = SEQ // rb
    halo_blocks = rb // CONV_HALO
    row = lambda b, c: b * steps + c
    const = lambda b, c: (0, 0)
    return pl.pallas_call(
        _ssd_kernel,
        grid=(bsz, steps),
        in_specs=[
            pl.BlockSpec((rb, SSM_CONV_DIM), lambda b, c: (row(b, c), XBC_OFF // SSM_CONV_DIM)),
            pl.BlockSpec((CONV_HALO, SSM_CONV_DIM),
                         lambda b, c: (jnp.maximum(row(b, c) * halo_blocks - 1, 0), XBC_OFF // SSM_CONV_DIM)),
            pl.BlockSpec((rb, SSM_INNER), lambda b, c: (row(b, c), Z_OFF // SSM_INNER)),
            pl.BlockSpec((rb, DT_PAD), lambda b, c: (row(b, c), 0)),
            pl.BlockSpec((SSM_CONV, SSM_CONV_DIM), const),
            pl.BlockSpec((1, SSM_CONV_DIM), const),
            pl.BlockSpec((1, DT_PAD), const),
            pl.BlockSpec((1, DT_PAD), const),
            pl.BlockSpec((1, SSM_INNER), const),
            pl.BlockSpec((1, SSM_INNER), const),
        ],
        out_specs=pl.BlockSpec((rb, SSM_INNER), lambda b, c: (row(b, c), 0)),
        out_shape=jax.ShapeDtypeStruct((t, SSM_INNER), bf16),
        scratch_shapes=[
            pltpu.VMEM((SSM_GROUPS, SSM_STATE, GROUP_COLS), f32),
            pltpu.VMEM((rb, SSM_INNER), f32),
            pltpu.VMEM((rb, SSM_GROUPS * SSM_STATE), bf16),
            pltpu.VMEM((rb, SSM_GROUPS * SSM_STATE), bf16),
            pltpu.VMEM((rb, SSM_INNER), f32),
            pltpu.VMEM((rb, SSM_INNER), f32),
            pltpu.VMEM((LANES, SSM_INNER), bf16),
        ],
        compiler_params=_params("parallel", "arbitrary"),
        name="ssd",
    )(proj, proj, proj, dtr, conv_w, conv_b, dt_bias, a_log, d_skip, norm_g)


def _merge_kernel(a_ref, b_ref, c_ref, wa_ref, wb_ref, wc_ref, ga_ref, gb_ref, gc_ref, o_ref):
    ya = jnp.dot(a_ref[...], wa_ref[0], preferred_element_type=f32)
    yb = jnp.dot(b_ref[...], wb_ref[0], preferred_element_type=f32)
    yc = jnp.dot(c_ref[...], wc_ref[0], preferred_element_type=f32)
    merged = (jax.nn.sigmoid(ga_ref[...]) * ya + jax.nn.sigmoid(gb_ref[...]) * yb
              + jax.nn.sigmoid(gc_ref[...]) * yc)
    o_ref[...] = merged.astype(o_ref.dtype)


def _tile_grid(n_row, n_col, cols_outer):
    if cols_outer:
        return (n_col, n_row), lambda f: (lambda j, i: f(i, j))
    return (n_row, n_col), lambda f: f


def _merge(ma, gb_, yn, proj, wa, wb, wc, layer, tm=512, tn=1024, cols_outer=True):
    t = proj.shape[0]
    d = D_MODEL
    gate_blk = GATE_OFF // tn
    per_gate = d // tn
    grid, im = _tile_grid(t // tm, d // tn, cols_outer)
    return pl.pallas_call(
        _merge_kernel,
        grid=grid,
        in_specs=[
            pl.BlockSpec((tm, POOL_WIDTH), im(lambda i, j: (i, 0))),
            pl.BlockSpec((tm, GMLP_WIDTH), im(lambda i, j: (i, 0))),
            pl.BlockSpec((tm, SSM_INNER), im(lambda i, j: (i, 0))),
            pl.BlockSpec((1, POOL_WIDTH, tn), im(lambda i, j: (layer, 0, j))),
            pl.BlockSpec((1, GMLP_WIDTH, tn), im(lambda i, j: (layer, 0, j))),
            pl.BlockSpec((1, SSM_INNER, tn), im(lambda i, j: (layer, 0, j))),
            pl.BlockSpec((tm, tn), im(lambda i, j: (i, gate_blk + j))),
            pl.BlockSpec((tm, tn), im(lambda i, j: (i, gate_blk + per_gate + j))),
            pl.BlockSpec((tm, tn), im(lambda i, j: (i, gate_blk + 2 * per_gate + j))),
        ],
        out_specs=pl.BlockSpec((tm, tn), im(lambda i, j: (i, j))),
        out_shape=jax.ShapeDtypeStruct((t, d), bf16),
        compiler_params=_params("arbitrary", "arbitrary"),
        name="merge",
    )(ma, gb_, yn, wa, wb, wc, proj, proj, proj)


def _wo_kernel(m_ref, w_ref, x_ref, g_ref, o_ref):
    o_ref[...] = x_ref[...] + g_ref[0] * jnp.dot(m_ref[...], w_ref[0], preferred_element_type=f32)


def _wo(merged, w_o, xf, mod, layer, tm=512, tn=2048, cols_outer=True):
    t, d = xf.shape
    tiles_per_batch = SEQ // tm
    grid, im = _tile_grid(t // tm, d // tn, cols_outer)
    return pl.pallas_call(
        _wo_kernel,
        grid=grid,
        in_specs=[
            pl.BlockSpec((tm, d), im(lambda i, j: (i, 0))),
            pl.BlockSpec((1, d, tn), im(lambda i, j: (layer, 0, j))),
            pl.BlockSpec((tm, tn), im(lambda i, j: (i, j))),
            pl.BlockSpec((1, 1, tn), im(lambda i, j: ((i // tiles_per_batch) * N_ADA + G1, 0, j))),
        ],
        out_specs=pl.BlockSpec((tm, tn), im(lambda i, j: (i, j))),
        out_shape=jax.ShapeDtypeStruct((t, d), f32),
        compiler_params=_params("arbitrary", "arbitrary"),
        name="wo",
    )(merged, w_o, xf, mod)


def _mlp_kernel(x_ref, sh_ref, sc_ref, g_ref, wu_ref, wd_ref, fn_ref, o_ref, h_ref, *, final_norm):
    f = pl.program_id(1)

    @pl.when(f == 0)
    def _():
        _mod_rmsnorm(x_ref, sh_ref, sc_ref, h_ref)
        o_ref[...] = jnp.zeros_like(o_ref)

    u = jnp.dot(h_ref[...], wu_ref[0], preferred_element_type=f32)
    u = jnp.square(jnp.maximum(u, 0.0)).astype(bf16)
    o_ref[...] += jnp.dot(u, wd_ref[0], preferred_element_type=f32)

    @pl.when(f == pl.num_programs(1) - 1)
    def _():
        rows = 256
        gate = g_ref[0]

        def body(r, carry):
            rs = pl.ds(pl.multiple_of(r * rows, rows), rows)
            xn = x_ref[rs, :] + gate * o_ref[rs, :]
            if final_norm:
                ms = jnp.mean(xn * xn, axis=-1, keepdims=True)
                xn = (xn * lax.rsqrt(ms + RMS_EPS)) * fn_ref[...]
            o_ref[rs, :] = xn
            return carry

        lax.fori_loop(0, x_ref.shape[0] // rows, body, 0)


def _mlp(xf, mod, w_up, w_down, layer, fn, final_norm, tm=1024, tf=512):
    t, d = xf.shape
    tiles_per_batch = SEQ // tm
    mod_spec = lambda k: pl.BlockSpec((1, 1, d), lambda i, f: ((i // tiles_per_batch) * N_ADA + k, 0, 0))
    return pl.pallas_call(
        functools.partial(_mlp_kernel, final_norm=final_norm),
        grid=(t // tm, D_FF // tf),
        in_specs=[
            pl.BlockSpec((tm, d), lambda i, f: (i, 0)),
            mod_spec(SH2),
            mod_spec(SC2),
            mod_spec(G2),
            pl.BlockSpec((1, d, tf), lambda i, f: (layer, 0, f)),
            pl.BlockSpec((1, tf, d), lambda i, f: (layer, f, 0)),
            pl.BlockSpec((1, d), lambda i, f: (0, 0)),
        ],
        out_specs=pl.BlockSpec((tm, d), lambda i, f: (i, 0)),
        out_shape=jax.ShapeDtypeStruct((t, d), f32),
        scratch_shapes=[pltpu.VMEM((tm, d), bf16)],
        compiler_params=_params("parallel", "arbitrary"),
        name="mlp",
    )(xf, mod, mod, mod, w_up, w_down, fn)


WIN_TILE = 512
W_ALIGN = 32


def _win_src_row(j):
    c = j * WIN_TILE
    src = jnp.where(c < GATE_OFF, c - XBC_OFF + W_XBC,
                    jnp.where(c < UV_OFF, c - GATE_OFF + W_GATE,
                              jnp.where(c < Z_OFF, c - UV_OFF + W_UV,
                                        jnp.where(c < P_OFF, c - Z_OFF + W_Z, c - P_OFF))))
    return pl.multiple_of(src, W_ALIGN)


def _win_kernel(wt_ref, o_ref):
    o_ref[...] = wt_ref[0].T.astype(o_ref.dtype)


def _win_prep(w_in_t, layer):
    _, n, d = w_in_t.shape
    assert n == W_END and all(o % WIN_TILE == 0 for o in (XBC_OFF, GATE_OFF, UV_OFF, Z_OFF, P_OFF))
    return pl.pallas_call(
        _win_kernel,
        grid=(N_MAIN // WIN_TILE,),
        in_specs=[pl.BlockSpec((pl.Element(1), pl.Element(WIN_TILE), pl.Element(d)),
                               lambda j: (layer, _win_src_row(j), 0))],
        out_specs=pl.BlockSpec((d, WIN_TILE), lambda j: (0, j)),
        out_shape=jax.ShapeDtypeStruct((d, N_MAIN), bf16),
        compiler_params=_params("parallel"),
        name="winprep",
    )(w_in_t)


def _pad_lanes(v):
    return jnp.pad(v.astype(f32), (0, DT_PAD - v.shape[0])).reshape(1, DT_PAD)


def kernel(x, c, w_ada, b_ada, w_in, pool_w, pool_scale, gmlp_ln_g, gmlp_ln_b, gmlp_ws, gmlp_bs, conv_w, conv_b, dt_bias, a_log, d_skip, ssm_norm, w_pool_out, w_gmlp_out, w_ssm_out, w_o, w_up, w_down, final_norm):
    bsz, seq, d = x.shape
    assert (seq, d) == (SEQ, D_MODEL) and w_ada.shape[0] == DEPTH
    xf = x.reshape(bsz * seq, d)
    fn = final_norm.reshape(1, d)
    w_in_t = jnp.swapaxes(w_in, 1, 2)
    w_a, w_b, w_c = w_pool_out.astype(bf16), w_gmlp_out.astype(bf16), w_ssm_out.astype(bf16)
    w_out, w_up_b, w_down_b = w_o.astype(bf16), w_up.astype(bf16), w_down.astype(bf16)
    for l in range(DEPTH):
        mod = _ada(c, w_ada, b_ada, l, tk=256).reshape(bsz * N_ADA, 1, d)
        w_main = _win_prep(w_in_t, l)
        proj, dtr = _inproj(xf, mod, w_main, w_in_t, l)
        rows = 2048 if l == 0 else 1024
        mixed_a = _pool(proj, pool_w[l].astype(bf16), pool_scale[l].reshape(1, POOL_WIDTH), ts=rows)
        gated_b = _gmlp(proj, gmlp_ln_g[l].reshape(1, GMLP_WIDTH), gmlp_ln_b[l].reshape(1, GMLP_WIDTH),
                        gmlp_ws[l], gmlp_bs[l].reshape(GMLP_HEADS, CHUNK, 1), tg=rows)
        yn = _ssd(proj, dtr, conv_w[l], conv_b[l].reshape(1, SSM_CONV_DIM), _pad_lanes(dt_bias[l]),
                  _pad_lanes(a_log[l]), jnp.repeat(d_skip[l], SSM_HEAD_DIM).reshape(1, SSM_INNER),
                  ssm_norm[l].reshape(1, SSM_INNER), bsz)
        merged = _merge(mixed_a, gated_b, yn, proj, w_a, w_b, w_c, l)
        xf = _wo(merged, w_out, xf, mod, l)
        xf = _mlp(xf, mod, w_up_b, w_down_b, l, fn, final_norm=(l == DEPTH - 1))
    return xf.reshape(bsz, seq, d)
```

```python
import functools

import jax
import jax.numpy as jnp
from jax import lax
from jax.experimental import pallas as pl
from jax.experimental.pallas import tpu as pltpu

f32 = jnp.float32
bf16 = jnp.bfloat16

D_MODEL = 2048
SEQ = 2048
DEPTH = 2
POOL_WIDTH = 1024
POOL_WINDOWS = (2, 4, 8, 16)
POOL_GROUP_DIM = 256
POOL_HALO = 16
GMLP_WIDTH = 1024
GMLP_HEADS = 8
GMLP_HEAD_DIM = 128
CHUNK = 128
SSM_INNER = 2048
SSM_HEAD_DIM = 64
SSM_HEADS = 32
SSM_GROUPS = 8
SSM_STATE = 128
SSM_CONV = 4
CONV_HALO = 8
SSM_CONV_DIM = 4096
GROUP_COLS = SSM_INNER // SSM_GROUPS
N_ADA = 6
D_FF = 8192
RMS_EPS = 1e-6
LN_EPS = 1e-5
LANES = 128
DT_PAD = LANES

XBC_OFF = 0
GATE_OFF = 4096
UV_OFF = 10240
Z_OFF = 12288
P_OFF = 14336
N_MAIN = 15360
W_UV, W_Z, W_XBC, W_DT, W_GATE, W_END = 1024, 3072, 5120, 9216, 9248, 15392

VMEM_LIMIT = 56 * 1024 * 1024

SH1, SC1, G1, SH2, SC2, G2 = range(N_ADA)


def _params(*sem):
    return pltpu.CompilerParams(dimension_semantics=sem, vmem_limit_bytes=VMEM_LIMIT)


def _split_bf16(x, n):
    parts, r = [], x
    for _ in range(n):
        p = r.astype(bf16)
        parts.append(p)
        r = r - p.astype(f32)
    return parts


def _mod_rmsnorm(x_ref, sh_ref, sc_ref, h_ref, rows=256):
    sh = sh_ref[0]
    sc1p = 1.0 + sc_ref[0]

    def body(r, carry):
        rs = pl.ds(pl.multiple_of(r * rows, rows), rows)
        xf = x_ref[rs, :]
        ms = jnp.mean(xf * xf, axis=-1, keepdims=True)
        h_ref[rs, :] = ((xf * lax.rsqrt(ms + RMS_EPS)) * sc1p + sh).astype(h_ref.dtype)
        return carry

    lax.fori_loop(0, x_ref.shape[0] // rows, body, 0)


def _ada_kernel(c_ref, w_ref, b_ref, o_ref):
    c_act = jax.nn.silu(c_ref[...]).astype(bf16)
    part = jnp.dot(c_act, w_ref[0].astype(bf16), preferred_element_type=f32)

    @pl.when(pl.program_id(0) == 0)
    def _():
        o_ref[...] = part + b_ref[0]

    @pl.when(pl.program_id(0) > 0)
    def _():
        o_ref[...] += part


def _ada(c, w_ada, b_ada, layer, tk):
    depth, d, n = w_ada.shape
    bsz = c.shape[0]
    return pl.pallas_call(
        _ada_kernel,
        grid=(d // tk,),
        in_specs=[
            pl.BlockSpec((bsz, tk), lambda k: (0, k)),
            pl.BlockSpec((1, tk, n), lambda k: (layer, k, 0)),
            pl.BlockSpec((1, 1, n), lambda k: (layer, 0, 0)),
        ],
        out_specs=pl.BlockSpec((bsz, n), lambda k: (0, 0)),
        out_shape=jax.ShapeDtypeStruct((bsz, n), f32),
        compiler_params=_params("arbitrary"),
        name="ada",
    )(c, w_ada, b_ada.reshape(depth, 1, n))


def _inproj_kernel(x_ref, sh_ref, sc_ref, w_ref, wdt_ref, o_ref, odt_ref, h_ref):
    @pl.when(pl.program_id(1) == 0)
    def _():
        _mod_rmsnorm(x_ref, sh_ref, sc_ref, h_ref)
        dt_all = lax.dot_general(h_ref[...], wdt_ref[0].astype(bf16), (((1,), (1,)), ((), ())),
                                 preferred_element_type=f32)
        lane = lax.broadcasted_iota(jnp.int32, dt_all.shape, 1)
        odt_ref[...] = jnp.where(lane < SSM_HEADS, dt_all, 0.0)

    o_ref[...] = jnp.dot(h_ref[...], w_ref[...], preferred_element_type=f32)


def _inproj(xf, mod, w_main, w_in_t, layer, tm=1024, tn=1536):
    t, d = xf.shape
    tiles_per_seq = SEQ // tm
    return pl.pallas_call(
        _inproj_kernel,
        grid=(t // tm, N_MAIN // tn),
        in_specs=[
            pl.BlockSpec((tm, d), lambda i, j: (i, 0)),
            pl.BlockSpec((1, 1, d), lambda i, j: ((i // tiles_per_seq) * N_ADA + SH1, 0, 0)),
            pl.BlockSpec((1, 1, d), lambda i, j: ((i // tiles_per_seq) * N_ADA + SC1, 0, 0)),
            pl.BlockSpec((d, tn), lambda i, j: (0, j)),
            pl.BlockSpec((pl.Element(1), pl.Element(DT_PAD), pl.Element(d)), lambda i, j: (layer, W_DT, 0)),
        ],
        out_specs=[
            pl.BlockSpec((tm, tn), lambda i, j: (i, j)),
            pl.BlockSpec((tm, DT_PAD), lambda i, j: (i, 0)),
        ],
        out_shape=[
            jax.ShapeDtypeStruct((t, N_MAIN), f32),
            jax.ShapeDtypeStruct((t, DT_PAD), f32),
        ],
        scratch_shapes=[pltpu.VMEM((tm, d), bf16)],
        compiler_params=_params("parallel", "arbitrary"),
        name="inproj",
    )(xf, mod, mod, w_main, w_in_t)


def _pool_kernel(p_ref, halo_ref, pw_ref, scale_ref, o_ref, *, tiles_per_seq):
    ts = p_ref.shape[0]
    tile_in_seq = pl.program_id(0) % tiles_per_seq
    first = tile_in_seq == 0
    pos = lax.broadcasted_iota(jnp.int32, (ts, 1), 0) + tile_in_seq * ts
    for g, w in enumerate(POOL_WINDOWS):
        cols = slice(g * POOL_GROUP_DIM, (g + 1) * POOL_GROUP_DIM)
        pg = p_ref[:, cols]
        halo = jnp.where(first, 0.0, halo_ref[:, cols])
        s = jnp.concatenate([halo, pg], axis=0)
        k = 1
        while k < w:
            s = s + pltpu.roll(s, k, axis=0)
            k *= 2
        count = jnp.minimum(pos + 1, w).astype(f32)
        pooled = s[POOL_HALO:, :] / count - pg
        mixed = jnp.dot(pooled.astype(bf16), pw_ref[g], preferred_element_type=f32)
        o_ref[:, cols] = (mixed * scale_ref[:, cols]).astype(o_ref.dtype)


def _pool(proj, pool_w, pool_scale, ts=2048):
    t = proj.shape[0]
    halo_blocks = ts // POOL_HALO
    p_blk = P_OFF // POOL_WIDTH
    return pl.pallas_call(
        functools.partial(_pool_kernel, tiles_per_seq=SEQ // ts),
        grid=(t // ts,),
        in_specs=[
            pl.BlockSpec((ts, POOL_WIDTH), lambda i: (i, p_blk)),
            pl.BlockSpec((POOL_HALO, POOL_WIDTH), lambda i: (jnp.maximum(i * halo_blocks - 1, 0), p_blk)),
            pl.BlockSpec(pool_w.shape, lambda i: (0, 0, 0)),
            pl.BlockSpec((1, POOL_WIDTH), lambda i: (0, 0)),
        ],
        out_specs=pl.BlockSpec((ts, POOL_WIDTH), lambda i: (i, 0)),
        out_shape=jax.ShapeDtypeStruct((t, POOL_WIDTH), bf16),
        compiler_params=_params("parallel"),
        name="pool",
    )(proj, proj, pool_w, pool_scale)


def _gelu(x):
    return 0.5 * x * (1.0 + lax.erf(x * 0.7071067811865476))


def _gmlp_kernel(u_ref, v_ref, g_ref, b_ref, ws_ref, bs_ref, o_ref, vn_ref):
    tg = u_ref.shape[0]
    v = _gelu(v_ref[...])
    mu = jnp.mean(v, axis=-1, keepdims=True)
    dv = v - mu
    var = jnp.mean(dv * dv, axis=-1, keepdims=True)
    vn_ref[...] = ((dv * lax.rsqrt(var + LN_EPS)) * g_ref[...] + b_ref[...]).astype(vn_ref.dtype)
    ri = lax.broadcasted_iota(jnp.int32, (CHUNK, CHUNK), 0)
    ci = lax.broadcasted_iota(jnp.int32, (CHUNK, CHUNK), 1)
    causal = ri >= ci
    for h in range(GMLP_HEADS):
        cols = slice(h * GMLP_HEAD_DIM, (h + 1) * GMLP_HEAD_DIM)
        wm = jnp.where(causal, ws_ref[h], 0.0).astype(bf16)
        bias = bs_ref[h]
        for c in range(tg // CHUNK):
            rows = slice(c * CHUNK, (c + 1) * CHUNK)
            mixed = jnp.dot(wm, vn_ref[rows, cols], preferred_element_type=f32) + bias
            o_ref[rows, cols] = (_gelu(u_ref[rows, cols]) * mixed).astype(o_ref.dtype)


def _gmlp(proj, ln_g, ln_b, ws, bs, tg=1024):
    t = proj.shape[0]
    u_blk = UV_OFF // GMLP_WIDTH
    return pl.pallas_call(
        _gmlp_kernel,
        grid=(t // tg,),
        in_specs=[
            pl.BlockSpec((tg, GMLP_WIDTH), lambda i: (i, u_blk)),
            pl.BlockSpec((tg, GMLP_WIDTH), lambda i: (i, u_blk + 1)),
            pl.BlockSpec((1, GMLP_WIDTH), lambda i: (0, 0)),
            pl.BlockSpec((1, GMLP_WIDTH), lambda i: (0, 0)),
            pl.BlockSpec((GMLP_HEADS, CHUNK, CHUNK), lambda i: (0, 0, 0)),
            pl.BlockSpec((GMLP_HEADS, CHUNK, 1), lambda i: (0, 0, 0)),
        ],
        out_specs=pl.BlockSpec((tg, GMLP_WIDTH), lambda i: (i, 0)),
        out_shape=jax.ShapeDtypeStruct((t, GMLP_WIDTH), bf16),
        scratch_shapes=[pltpu.VMEM((tg, GMLP_WIDTH), bf16)],
        compiler_params=_params("parallel"),
        name="gmlp",
    )(proj, proj, ln_g, ln_b, ws, bs)


def _ssd_chunk(rows, dtr_ref, dtb_ref, alog_ref, z_ref, dskip_ref, ng_ref, o_ref, state_ref, xs_ref,
               bm_ref, cm_ref, wdt_ref, eout_ref, consts):
    causal, tril, expand, low_half = consts
    dt = jax.nn.softplus(dtr_ref[rows, :] + dtb_ref[...])
    da = dt * (-jnp.exp(alog_ref[...]))
    cs3 = jnp.dot(tril, jnp.concatenate(_split_bf16(da, 3), axis=1), preferred_element_type=f32)
    a_cs = cs3[:, :LANES] + cs3[:, LANES:2 * LANES] + cs3[:, 2 * LANES:]
    a_last = a_cs[CHUNK - 1:CHUNK, :]
    wdt = dt * jnp.exp(a_last - a_cs)
    e_out = jnp.exp(a_cs)
    a_cs_t = a_cs.T
    dt_t = dt.T
    pieces = jnp.concatenate(_split_bf16(wdt, 3) + _split_bf16(e_out, 3), axis=0)
    ex = jnp.dot(pieces, expand, preferred_element_type=f32)
    wdt_ref[rows, :] = ex[:CHUNK] + ex[CHUNK:2 * CHUNK] + ex[2 * CHUNK:3 * CHUNK]
    eout_ref[rows, :] = ex[3 * CHUNK:4 * CHUNK] + ex[4 * CHUNK:5 * CHUNK] + ex[5 * CHUNK:]
    last = rows.start + CHUNK - 1

    for g in range(SSM_GROUPS):
        gcols = slice(g * GROUP_COLS, (g + 1) * GROUP_COLS)
        ncols = slice(g * SSM_STATE, (g + 1) * SSM_STATE)
        bg = bm_ref[rows, ncols]
        cg = cm_ref[rows, ncols]
        cb = lax.dot_general(cg, bg, (((1,), (1,)), ((), ())), preferred_element_type=f32)
        y_pairs = []
        for pr in range(2):
            scores = []
            for hh in range(2):
                h = 4 * g + 2 * pr + hh
                diff = a_cs[:, h:h + 1] - a_cs_t[h:h + 1, :]
                decay = jnp.exp(jnp.where(causal, diff, -jnp.inf))
                scores.append((cb * decay * dt_t[h:h + 1, :]).astype(bf16))
            lhs = jnp.concatenate(scores, axis=1)
            xp = xs_ref[rows, g * GROUP_COLS + pr * LANES:g * GROUP_COLS + (pr + 1) * LANES]
            rhs = jnp.concatenate([jnp.where(low_half, xp, 0.0), jnp.where(low_half, 0.0, xp)], axis=0)
            y_pairs.append(jnp.dot(lhs, rhs.astype(bf16), preferred_element_type=f32))
        y = jnp.concatenate(y_pairs, axis=1)
        xs_g = xs_ref[rows, gcols]
        xdw = (xs_g * wdt_ref[rows, gcols]).astype(bf16)
        st_new = lax.dot_general(bg, xdw, (((0,), (0,)), ((), ())), preferred_element_type=f32)
        prev = state_ref[g]
        y = y + jnp.dot(cg, prev.astype(bf16), preferred_element_type=f32) * eout_ref[rows, gcols]
        state_ref[g] = prev * eout_ref[last:last + 1, gcols] + st_new
        y = y + dskip_ref[:, gcols] * xs_g
        y = y * jax.nn.silu(z_ref[rows, gcols])
        y = y * lax.rsqrt(jnp.mean(y * y, axis=-1, keepdims=True) + RMS_EPS)
        o_ref[rows, gcols] = (y * ng_ref[:, gcols]).astype(o_ref.dtype)


def _ssd_kernel(xbc_ref, halo_ref, z_ref, dtr_ref, cw_ref, cb_ref, dtb_ref, alog_ref, dskip_ref,
                ng_ref, o_ref, state_ref, xs_ref, bm_ref, cm_ref, wdt_ref, eout_ref, expand_ref):
    step = pl.program_id(1)

    @pl.when(step == 0)
    def _():
        state_ref[...] = jnp.zeros_like(state_ref)
        hi = lax.broadcasted_iota(jnp.int32, (LANES, SSM_INNER), 0)
        ch = lax.broadcasted_iota(jnp.int32, (LANES, SSM_INNER), 1)
        expand_ref[...] = jnp.where(hi == ch // SSM_HEAD_DIM, 1.0, 0.0).astype(bf16)

    slab, rc = 256, 64
    n_state = SSM_GROUPS * SSM_STATE
    for s in range(SSM_CONV_DIM // slab):
        cols = slice(s * slab, (s + 1) * slab)
        w = cw_ref[:, cols]
        bias = cb_ref[:, cols]
        for r in range(xbc_ref.shape[0] // rc):
            rows = slice(r * rc, (r + 1) * rc)
            if r == 0:
                halo = jnp.where(step == 0, 0.0, halo_ref[:, cols])
            else:
                halo = xbc_ref[r * rc - CONV_HALO:r * rc, cols]
            cur = xbc_ref[rows, cols]
            e = jnp.concatenate([halo, cur], axis=0)
            acc = cur * w[SSM_CONV - 1:SSM_CONV, :]
            for k in range(1, SSM_CONV):
                acc = acc + pltpu.roll(e, k, axis=0)[CONV_HALO:, :] * w[SSM_CONV - 1 - k:SSM_CONV - k, :]
            act = jax.nn.silu(acc + bias)
            if s * slab < SSM_INNER:
                xs_ref[rows, cols] = act
            elif s * slab < SSM_INNER + n_state:
                off = s * slab - SSM_INNER
                bm_ref[rows, off:off + slab] = act.astype(bf16)
            else:
                off = s * slab - SSM_INNER - n_state
                cm_ref[rows, off:off + slab] = act.astype(bf16)

    ri = lax.broadcasted_iota(jnp.int32, (CHUNK, CHUNK), 0)
    ci = lax.broadcasted_iota(jnp.int32, (CHUNK, CHUNK), 1)
    causal = ri >= ci
    tril = jnp.where(causal, 1.0, 0.0).astype(bf16)
    low_half = lax.broadcasted_iota(jnp.int32, (CHUNK, LANES), 1) < SSM_HEAD_DIM
    consts = (causal, tril, expand_ref[...], low_half)
    for cc in range(xbc_ref.shape[0] // CHUNK):
        _ssd_chunk(slice(cc * CHUNK, (cc + 1) * CHUNK), dtr_ref, dtb_ref, alog_ref, z_ref, dskip_ref,
                   ng_ref, o_ref, state_ref, xs_ref, bm_ref, cm_ref, wdt_ref, eout_ref, consts)


def _ssd(proj, dtr, conv_w, conv_b, dt_bias, a_log, d_skip, norm_g, bsz, chunks_per_step=2):
    t = proj.shape[0]
    rb = chunks_per_step * CHUNK
    steps = SEQ // rb
    halo_blocks = rb // CONV_HALO
    row = lambda b, c: b * steps + c
    const = lambda b, c: (0, 0)
    return pl.pallas_call(
        _ssd_kernel,
        grid=(bsz, steps),
        in_specs=[
            pl.BlockSpec((rb, SSM_CONV_DIM), lambda b, c: (row(b, c), XBC_OFF // SSM_CONV_DIM)),
            pl.BlockSpec((CONV_HALO, SSM_CONV_DIM),
                         lambda b, c: (jnp.maximum(row(b, c) * halo_blocks - 1, 0), XBC_OFF // SSM_CONV_DIM)),
            pl.BlockSpec((rb, SSM_INNER), lambda b, c: (row(b, c), Z_OFF // SSM_INNER)),
            pl.BlockSpec((rb, DT_PAD), lambda b, c: (row(b, c), 0)),
            pl.BlockSpec((SSM_CONV, SSM_CONV_DIM), const),
            pl.BlockSpec((1, SSM_CONV_DIM), const),
            pl.BlockSpec((1, DT_PAD), const),
            pl.BlockSpec((1, DT_PAD), const),
            pl.BlockSpec((1, SSM_INNER), const),
            pl.BlockSpec((1, SSM_INNER), const),
        ],
        out_specs=pl.BlockSpec((rb, SSM_INNER), lambda b, c: (row(b, c), 0)),
        out_shape=jax.ShapeDtypeStruct((t, SSM_INNER), bf16),
        scratch_shapes=[
            pltpu.VMEM((SSM_GROUPS, SSM_STATE, GROUP_COLS), f32),
            pltpu.VMEM((rb, SSM_INNER), f32),
            pltpu.VMEM((rb, SSM_GROUPS * SSM_STATE), bf16),
            pltpu.VMEM((rb, SSM_GROUPS * SSM_STATE), bf16),
            pltpu.VMEM((rb, SSM_INNER), f32),
            pltpu.VMEM((rb, SSM_INNER), f32),
            pltpu.VMEM((LANES, SSM_INNER), bf16),
        ],
        compiler_params=_params("parallel", "arbitrary"),
        name="ssd",
    )(proj, proj, proj, dtr, conv_w, conv_b, dt_bias, a_log, d_skip, norm_g)


def _merge_kernel(a_ref, b_ref, c_ref, wa_ref, wb_ref, wc_ref, ga_ref, gb_ref, gc_ref, o_ref):
    ya = jnp.dot(a_ref[...], wa_ref[0], preferred_element_type=f32)
    yb = jnp.dot(b_ref[...], wb_ref[0], preferred_element_type=f32)
    yc = jnp.dot(c_ref[...], wc_ref[0], preferred_element_type=f32)
    merged = (jax.nn.sigmoid(ga_ref[...]) * ya + jax.nn.sigmoid(gb_ref[...]) * yb
              + jax.nn.sigmoid(gc_ref[...]) * yc)
    o_ref[...] = merged.astype(o_ref.dtype)


def _tile_grid(n_row, n_col, cols_outer):
    if cols_outer:
        return (n_col, n_row), lambda f: (lambda j, i: f(i, j))
    return (n_row, n_col), lambda f: f


def _merge(ma, gb_, yn, proj, wa, wb, wc, layer, tm=512, tn=1024, cols_outer=True):
    t = proj.shape[0]
    d = D_MODEL
    gate_blk = GATE_OFF // tn
    per_gate = d // tn
    grid, im = _tile_grid(t // tm, d // tn, cols_outer)
    return pl.pallas_call(
        _merge_kernel,
        grid=grid,
        in_specs=[
            pl.BlockSpec((tm, POOL_WIDTH), im(lambda i, j: (i, 0))),
            pl.BlockSpec((tm, GMLP_WIDTH), im(lambda i, j: (i, 0))),
            pl.BlockSpec((tm, SSM_INNER), im(lambda i, j: (i, 0))),
            pl.BlockSpec((1, POOL_WIDTH, tn), im(lambda i, j: (layer, 0, j))),
            pl.BlockSpec((1, GMLP_WIDTH, tn), im(lambda i, j: (layer, 0, j))),
            pl.BlockSpec((1, SSM_INNER, tn), im(lambda i, j: (layer, 0, j))),
            pl.BlockSpec((tm, tn), im(lambda i, j: (i, gate_blk + j))),
            pl.BlockSpec((tm, tn), im(lambda i, j: (i, gate_blk + per_gate + j))),
            pl.BlockSpec((tm, tn), im(lambda i, j: (i, gate_blk + 2 * per_gate + j))),
        ],
        out_specs=pl.BlockSpec((tm, tn), im(lambda i, j: (i, j))),
        out_shape=jax.ShapeDtypeStruct((t, d), bf16),
        compiler_params=_params("arbitrary", "arbitrary"),
        name="merge",
    )(ma, gb_, yn, wa, wb, wc, proj, proj, proj)


def _wo_kernel(m_ref, w_ref, x_ref, g_ref, o_ref):
    o_ref[...] = x_ref[...] + g_ref[0] * jnp.dot(m_ref[...], w_ref[0], preferred_element_type=f32)


def _wo(merged, w_o, xf, mod, layer, tm=512, tn=2048, cols_outer=True):
    t, d = xf.shape
    tiles_per_batch = SEQ // tm
    grid, im = _tile_grid(t // tm, d // tn, cols_outer)
    return pl.pallas_call(
        _wo_kernel,
        grid=grid,
        in_specs=[
            pl.BlockSpec((tm, d), im(lambda i, j: (i, 0))),
            pl.BlockSpec((1, d, tn), im(lambda i, j: (layer, 0, j))),
            pl.BlockSpec((tm, tn), im(lambda i, j: (i, j))),
            pl.BlockSpec((1, 1, tn), im(lambda i, j: ((i // tiles_per_batch) * N_ADA + G1, 0, j))),
        ],
        out_specs=pl.BlockSpec((tm, tn), im(lambda i, j: (i, j))),
        out_shape=jax.ShapeDtypeStruct((t, d), f32),
        compiler_params=_params("arbitrary", "arbitrary"),
        name="wo",
    )(merged, w_o, xf, mod)


def _mlp_kernel(x_ref, sh_ref, sc_ref, g_ref, wu_ref, wd_ref, fn_ref, o_ref, h_ref, *, final_norm):
    f = pl.program_id(1)

    @pl.when(f == 0)
    def _():
        _mod_rmsnorm(x_ref, sh_ref, sc_ref, h_ref)
        o_ref[...] = jnp.zeros_like(o_ref)

    u = jnp.dot(h_ref[...], wu_ref[0], preferred_element_type=f32)
    u = jnp.square(jnp.maximum(u, 0.0)).astype(bf16)
    o_ref[...] += jnp.dot(u, wd_ref[0], preferred_element_type=f32)

    @pl.when(f == pl.num_programs(1) - 1)
    def _():
        rows = 256
        gate = g_ref[0]

        def body(r, carry):
            rs = pl.ds(pl.multiple_of(r * rows, rows), rows)
            xn = x_ref[rs, :] + gate * o_ref[rs, :]
            if final_norm:
                ms = jnp.mean(xn * xn, axis=-1, keepdims=True)
                xn = (xn * lax.rsqrt(ms + RMS_EPS)) * fn_ref[...]
            o_ref[rs, :] = xn
            return carry

        lax.fori_loop(0, x_ref.shape[0] // rows, body, 0)


def _mlp(xf, mod, w_up, w_down, layer, fn, final_norm, tm=1024, tf=512):
    t, d = xf.shape
    tiles_per_batch = SEQ // tm
    mod_spec = lambda k: pl.BlockSpec((1, 1, d), lambda i, f: ((i // tiles_per_batch) * N_ADA + k, 0, 0))
    return pl.pallas_call(
        functools.partial(_mlp_kernel, final_norm=final_norm),
        grid=(t // tm, D_FF // tf),
        in_specs=[
            pl.BlockSpec((tm, d), lambda i, f: (i, 0)),
            mod_spec(SH2),
            mod_spec(SC2),
            mod_spec(G2),
            pl.BlockSpec((1, d, tf), lambda i, f: (layer, 0, f)),
            pl.BlockSpec((1, tf, d), lambda i, f: (layer, f, 0)),
            pl.BlockSpec((1, d), lambda i, f: (0, 0)),
        ],
        out_specs=pl.BlockSpec((tm, d), lambda i, f: (i, 0)),
        out_shape=jax.ShapeDtypeStruct((t, d), f32),
        scratch_shapes=[pltpu.VMEM((tm, d), bf16)],
        compiler_params=_params("parallel", "arbitrary"),
        name="mlp",
    )(xf, mod, mod, mod, w_up, w_down, fn)


WIN_TILE = 512
W_ALIGN = 32


def _win_src_row(j):
    c = j * WIN_TILE
    src = jnp.where(c < GATE_OFF, c - XBC_OFF + W_XBC,
                    jnp.where(c < UV_OFF, c - GATE_OFF + W_GATE,
                              jnp.where(c < Z_OFF, c - UV_OFF + W_UV,
                                        jnp.where(c < P_OFF, c - Z_OFF + W_Z, c - P_OFF))))
    return pl.multiple_of(src, W_ALIGN)


def _win_kernel(wt_ref, o_ref):
    o_ref[...] = wt_ref[0].T.astype(o_ref.dtype)


def _win_prep(w_in_t, layer):
    _, n, d = w_in_t.shape
    assert n == W_END and all(o % WIN_TILE == 0 for o in (XBC_OFF, GATE_OFF, UV_OFF, Z_OFF, P_OFF))
    return pl.pallas_call(
        _win_kernel,
        grid=(N_MAIN // WIN_TILE,),
        in_specs=[pl.BlockSpec((pl.Element(1), pl.Element(WIN_TILE), pl.Element(d)),
                               lambda j: (layer, _win_src_row(j), 0))],
        out_specs=pl.BlockSpec((d, WIN_TILE), lambda j: (0, j)),
        out_shape=jax.ShapeDtypeStruct((d, N_MAIN), bf16),
        compiler_params=_params("parallel"),
        name="winprep",
    )(w_in_t)


def _pad_lanes(v):
    return jnp.pad(v.astype(f32), (0, DT_PAD - v.shape[0])).reshape(1, DT_PAD)


def kernel(x, c, w_ada, b_ada, w_in, pool_w, pool_scale, gmlp_ln_g, gmlp_ln_b, gmlp_ws, gmlp_bs, conv_w, conv_b, dt_bias, a_log, d_skip, ssm_norm, w_pool_out, w_gmlp_out, w_ssm_out, w_o, w_up, w_down, final_norm):
    bsz, seq, d = x.shape
    assert (seq, d) == (SEQ, D_MODEL) and w_ada.shape[0] == DEPTH
    xf = x.reshape(bsz * seq, d)
    fn = final_norm.reshape(1, d)
    w_in_t = jnp.swapaxes(w_in, 1, 2)
    w_a, w_b, w_c = w_pool_out.astype(bf16), w_gmlp_out.astype(bf16), w_ssm_out.astype(bf16)
    w_out, w_up_b, w_down_b = w_o.astype(bf16), w_up.astype(bf16), w_down.astype(bf16)
    for l in range(DEPTH):
        mod = _ada(c, w_ada, b_ada, l, tk=256).reshape(bsz * N_ADA, 1, d)
        w_main = _win_prep(w_in_t, l)
        proj, dtr = _inproj(xf, mod, w_main, w_in_t, l)
        mixed_a = _pool(proj, pool_w[l].astype(bf16), pool_scale[l].reshape(1, POOL_WIDTH))
        gated_b = _gmlp(proj, gmlp_ln_g[l].reshape(1, GMLP_WIDTH), gmlp_ln_b[l].reshape(1, GMLP_WIDTH),
                        gmlp_ws[l], gmlp_bs[l].reshape(GMLP_HEADS, CHUNK, 1))
        yn = _ssd(proj, dtr, conv_w[l], conv_b[l].reshape(1, SSM_CONV_DIM), _pad_lanes(dt_bias[l]),
                  _pad_lanes(a_log[l]), jnp.repeat(d_skip[l], SSM_HEAD_DIM).reshape(1, SSM_INNER),
                  ssm_norm[l].reshape(1, SSM_INNER), bsz)
        merged = _merge(mixed_a, gated_b, yn, proj, w_a, w_b, w_c, l)
        xf = _wo(merged, w_out, xf, mod, l)
        xf = _mlp(xf, mod, w_up_b, w_down_b, l, fn, final_norm=(l == DEPTH - 1))
    return xf.reshape(bsz, seq, d)
```

```python
import functools

import jax
import jax.numpy as jnp
from jax import lax
from jax.experimental import pallas as pl
from jax.experimental.pallas import tpu as pltpu

f32 = jnp.float32
bf16 = jnp.bfloat16

D_MODEL = 2048
SEQ = 2048
DEPTH = 2
POOL_WIDTH = 1024
POOL_WINDOWS = (2, 4, 8, 16)
POOL_GROUP_DIM = 256
POOL_HALO = 16
GMLP_WIDTH = 1024
GMLP_HEADS = 8
GMLP_HEAD_DIM = 128
CHUNK = 128
SSM_INNER = 2048
SSM_HEAD_DIM = 64
SSM_HEADS = 32
SSM_GROUPS = 8
SSM_STATE = 128
SSM_CONV = 4
CONV_HALO = 8
SSM_CONV_DIM = 4096
GROUP_COLS = SSM_INNER // SSM_GROUPS
N_ADA = 6
D_FF = 8192
RMS_EPS = 1e-6
LN_EPS = 1e-5
LANES = 128
DT_PAD = LANES

XBC_OFF = 0
GATE_OFF = 4096
UV_OFF = 10240
Z_OFF = 12288
P_OFF = 14336
N_MAIN = 15360
W_UV, W_Z, W_XBC, W_DT, W_GATE, W_END = 1024, 3072, 5120, 9216, 9248, 15392

VMEM_LIMIT = 56 * 1024 * 1024

SH1, SC1, G1, SH2, SC2, G2 = range(N_ADA)


def _params(*sem):
    return pltpu.CompilerParams(dimension_semantics=sem, vmem_limit_bytes=VMEM_LIMIT)


def _split_bf16(x, n):
    parts, r = [], x
    for _ in range(n):
        p = r.astype(bf16)
        parts.append(p)
        r = r - p.astype(f32)
    return parts


def _mod_rmsnorm(x_ref, sh_ref, sc_ref, h_ref, rows=256):
    sh = sh_ref[0]
    sc1p = 1.0 + sc_ref[0]

    def body(r, carry):
        rs = pl.ds(pl.multiple_of(r * rows, rows), rows)
        xf = x_ref[rs, :]
        ms = jnp.mean(xf * xf, axis=-1, keepdims=True)
        h_ref[rs, :] = ((xf * lax.rsqrt(ms + RMS_EPS)) * sc1p + sh).astype(h_ref.dtype)
        return carry

    lax.fori_loop(0, x_ref.shape[0] // rows, body, 0)


def _ada_kernel(c_ref, w_ref, b_ref, o_ref):
    c_act = jax.nn.silu(c_ref[...]).astype(bf16)
    part = jnp.dot(c_act, w_ref[0].astype(bf16), preferred_element_type=f32)

    @pl.when(pl.program_id(0) == 0)
    def _():
        o_ref[...] = part + b_ref[0]

    @pl.when(pl.program_id(0) > 0)
    def _():
        o_ref[...] += part


def _ada(c, w_ada, b_ada, layer, tk):
    depth, d, n = w_ada.shape
    bsz = c.shape[0]
    return pl.pallas_call(
        _ada_kernel,
        grid=(d // tk,),
        in_specs=[
            pl.BlockSpec((bsz, tk), lambda k: (0, k)),
            pl.BlockSpec((1, tk, n), lambda k: (layer, k, 0)),
            pl.BlockSpec((1, 1, n), lambda k: (layer, 0, 0)),
        ],
        out_specs=pl.BlockSpec((bsz, n), lambda k: (0, 0)),
        out_shape=jax.ShapeDtypeStruct((bsz, n), f32),
        compiler_params=_params("arbitrary"),
        name="ada",
    )(c, w_ada, b_ada.reshape(depth, 1, n))


def _inproj_kernel(x_ref, sh_ref, sc_ref, w_ref, wdt_ref, o_ref, odt_ref, h_ref):
    @pl.when(pl.program_id(1) == 0)
    def _():
        _mod_rmsnorm(x_ref, sh_ref, sc_ref, h_ref)
        dt_all = lax.dot_general(h_ref[...], wdt_ref[0].astype(bf16), (((1,), (1,)), ((), ())),
                                 preferred_element_type=f32)
        lane = lax.broadcasted_iota(jnp.int32, dt_all.shape, 1)
        odt_ref[...] = jnp.where(lane < SSM_HEADS, dt_all, 0.0)

    o_ref[...] = jnp.dot(h_ref[...], w_ref[...], preferred_element_type=f32)


def _inproj(xf, mod, w_main, w_in_t, layer, tm=1024, tn=1536):
    t, d = xf.shape
    tiles_per_seq = SEQ // tm
    return pl.pallas_call(
        _inproj_kernel,
        grid=(t // tm, N_MAIN // tn),
        in_specs=[
            pl.BlockSpec((tm, d), lambda i, j: (i, 0)),
            pl.BlockSpec((1, 1, d), lambda i, j: ((i // tiles_per_seq) * N_ADA + SH1, 0, 0)),
            pl.BlockSpec((1, 1, d), lambda i, j: ((i // tiles_per_seq) * N_ADA + SC1, 0, 0)),
            pl.BlockSpec((d, tn), lambda i, j: (0, j)),
            pl.BlockSpec((pl.Element(1), pl.Element(DT_PAD), pl.Element(d)), lambda i, j: (layer, W_DT, 0)),
        ],
        out_specs=[
            pl.BlockSpec((tm, tn), lambda i, j: (i, j)),
            pl.BlockSpec((tm, DT_PAD), lambda i, j: (i, 0)),
        ],
        out_shape=[
            jax.ShapeDtypeStruct((t, N_MAIN), f32),
            jax.ShapeDtypeStruct((t, DT_PAD), f32),
        ],
        scratch_shapes=[pltpu.VMEM((tm, d), bf16)],
        compiler_params=_params("parallel", "arbitrary"),
        name="inproj",
    )(xf, mod, mod, w_main, w_in_t)


def _pool_kernel(p_ref, halo_ref, pw_ref, scale_ref, o_ref, *, tiles_per_seq):
    ts = p_ref.shape[0]
    tile_in_seq = pl.program_id(0) % tiles_per_seq
    first = tile_in_seq == 0
    pos = lax.broadcasted_iota(jnp.int32, (ts, 1), 0) + tile_in_seq * ts
    for g, w in enumerate(POOL_WINDOWS):
        cols = slice(g * POOL_GROUP_DIM, (g + 1) * POOL_GROUP_DIM)
        pg = p_ref[:, cols]
        halo = jnp.where(first, 0.0, halo_ref[:, cols])
        s = jnp.concatenate([halo, pg], axis=0)
        k = 1
        while k < w:
            s = s + pltpu.roll(s, k, axis=0)
            k *= 2
        count = jnp.minimum(pos + 1, w).astype(f32)
        pooled = s[POOL_HALO:, :] / count - pg
        mixed = jnp.dot(pooled.astype(bf16), pw_ref[g], preferred_element_type=f32)
        o_ref[:, cols] = (mixed * scale_ref[:, cols]).astype(o_ref.dtype)


def _pool(proj, pool_w, pool_scale, ts=2048):
    t = proj.shape[0]
    halo_blocks = ts // POOL_HALO
    p_blk = P_OFF // POOL_WIDTH
    return pl.pallas_call(
        functools.partial(_pool_kernel, tiles_per_seq=SEQ // ts),
        grid=(t // ts,),
        in_specs=[
            pl.BlockSpec((ts, POOL_WIDTH), lambda i: (i, p_blk)),
            pl.BlockSpec((POOL_HALO, POOL_WIDTH), lambda i: (jnp.maximum(i * halo_blocks - 1, 0), p_blk)),
            pl.BlockSpec(pool_w.shape, lambda i: (0, 0, 0)),
            pl.BlockSpec((1, POOL_WIDTH), lambda i: (0, 0)),
        ],
        out_specs=pl.BlockSpec((ts, POOL_WIDTH), lambda i: (i, 0)),
        out_shape=jax.ShapeDtypeStruct((t, POOL_WIDTH), bf16),
        compiler_params=_params("parallel"),
        name="pool",
    )(proj, proj, pool_w, pool_scale)


def _gelu(x):
    return 0.5 * x * (1.0 + lax.erf(x * 0.7071067811865476))


def _gmlp_kernel(u_ref, v_ref, g_ref, b_ref, ws_ref, bs_ref, o_ref, vn_ref):
    tg = u_ref.shape[0]
    v = _gelu(v_ref[...])
    mu = jnp.mean(v, axis=-1, keepdims=True)
    dv = v - mu
    var = jnp.mean(dv * dv, axis=-1, keepdims=True)
    vn_ref[...] = ((dv * lax.rsqrt(var + LN_EPS)) * g_ref[...] + b_ref[...]).astype(vn_ref.dtype)
    ri = lax.broadcasted_iota(jnp.int32, (CHUNK, CHUNK), 0)
    ci = lax.broadcasted_iota(jnp.int32, (CHUNK, CHUNK), 1)
    causal = ri >= ci
    for h in range(GMLP_HEADS):
        cols = slice(h * GMLP_HEAD_DIM, (h + 1) * GMLP_HEAD_DIM)
        wm = jnp.where(causal, ws_ref[h], 0.0).astype(bf16)
        bias = bs_ref[h]
        for c in range(tg // CHUNK):
            rows = slice(c * CHUNK, (c + 1) * CHUNK)
            mixed = jnp.dot(wm, vn_ref[rows, cols], preferred_element_type=f32) + bias
            o_ref[rows, cols] = (_gelu(u_ref[rows, cols]) * mixed).astype(o_ref.dtype)


def _gmlp(proj, ln_g, ln_b, ws, bs, tg=1024):
    t = proj.shape[0]
    u_blk = UV_OFF // GMLP_WIDTH
    return pl.pallas_call(
        _gmlp_kernel,
        grid=(t // tg,),
        in_specs=[
            pl.BlockSpec((tg, GMLP_WIDTH), lambda i: (i, u_blk)),
            pl.BlockSpec((tg, GMLP_WIDTH), lambda i: (i, u_blk + 1)),
            pl.BlockSpec((1, GMLP_WIDTH), lambda i: (0, 0)),
            pl.BlockSpec((1, GMLP_WIDTH), lambda i: (0, 0)),
            pl.BlockSpec((GMLP_HEADS, CHUNK, CHUNK), lambda i: (0, 0, 0)),
            pl.BlockSpec((GMLP_HEADS, CHUNK, 1), lambda i: (0, 0, 0)),
        ],
        out_specs=pl.BlockSpec((tg, GMLP_WIDTH), lambda i: (i, 0)),
        out_shape=jax.ShapeDtypeStruct((t, GMLP_WIDTH), bf16),
        scratch_shapes=[pltpu.VMEM((tg, GMLP_WIDTH), bf16)],
        compiler_params=_params("parallel"),
        name="gmlp",
    )(proj, proj, ln_g, ln_b, ws, bs)


def _gmlp_streamed(proj, ln_g, ln_b, ws, bs, tg=1024, in_buffers=3):
    t = proj.shape[0]
    u_blk = UV_OFF // GMLP_WIDTH

    def outer(proj_ref, g_ref, b_ref, ws_ref, bs_ref, o_ref, vn_ref):
        def step(u_ref, v_ref, out_ref):
            _gmlp_kernel(u_ref, v_ref, g_ref, b_ref, ws_ref, bs_ref, out_ref, vn_ref)

        pltpu.emit_pipeline(
            step,
            grid=(t // tg,),
            in_specs=[
                pl.BlockSpec((tg, GMLP_WIDTH), lambda i: (i, u_blk), pipeline_mode=pl.Buffered(in_buffers)),
                pl.BlockSpec((tg, GMLP_WIDTH), lambda i: (i, u_blk + 1), pipeline_mode=pl.Buffered(in_buffers)),
            ],
            out_specs=[pl.BlockSpec((tg, GMLP_WIDTH), lambda i: (i, 0))],
        )(proj_ref, proj_ref, o_ref)

    resident = pl.BlockSpec(memory_space=pltpu.VMEM)
    return pl.pallas_call(
        outer,
        in_specs=[pl.BlockSpec(memory_space=pl.ANY), resident, resident, resident, resident],
        out_specs=pl.BlockSpec(memory_space=pl.ANY),
        out_shape=jax.ShapeDtypeStruct((t, GMLP_WIDTH), bf16),
        scratch_shapes=[pltpu.VMEM((tg, GMLP_WIDTH), bf16)],
        compiler_params=pltpu.CompilerParams(vmem_limit_bytes=VMEM_LIMIT),
        name="gmlp",
    )(proj, ln_g, ln_b, ws, bs)


def _ssd_chunk(rows, dtr_ref, dtb_ref, alog_ref, z_ref, dskip_ref, ng_ref, o_ref, state_ref, xs_ref,
               bm_ref, cm_ref, wdt_ref, eout_ref, consts):
    causal, tril, expand, low_half = consts
    dt = jax.nn.softplus(dtr_ref[rows, :] + dtb_ref[...])
    da = dt * (-jnp.exp(alog_ref[...]))
    cs3 = jnp.dot(tril, jnp.concatenate(_split_bf16(da, 3), axis=1), preferred_element_type=f32)
    a_cs = cs3[:, :LANES] + cs3[:, LANES:2 * LANES] + cs3[:, 2 * LANES:]
    a_last = a_cs[CHUNK - 1:CHUNK, :]
    wdt = dt * jnp.exp(a_last - a_cs)
    e_out = jnp.exp(a_cs)
    a_cs_t = a_cs.T
    dt_t = dt.T
    pieces = jnp.concatenate(_split_bf16(wdt, 3) + _split_bf16(e_out, 3), axis=0)
    ex = jnp.dot(pieces, expand, preferred_element_type=f32)
    wdt_ref[rows, :] = ex[:CHUNK] + ex[CHUNK:2 * CHUNK] + ex[2 * CHUNK:3 * CHUNK]
    eout_ref[rows, :] = ex[3 * CHUNK:4 * CHUNK] + ex[4 * CHUNK:5 * CHUNK] + ex[5 * CHUNK:]
    last = rows.start + CHUNK - 1

    for g in range(SSM_GROUPS):
        gcols = slice(g * GROUP_COLS, (g + 1) * GROUP_COLS)
        ncols = slice(g * SSM_STATE, (g + 1) * SSM_STATE)
        bg = bm_ref[rows, ncols]
        cg = cm_ref[rows, ncols]
        cb = lax.dot_general(cg, bg, (((1,), (1,)), ((), ())), preferred_element_type=f32)
        y_pairs = []
        for pr in range(2):
            scores = []
            for hh in range(2):
                h = 4 * g + 2 * pr + hh
                diff = a_cs[:, h:h + 1] - a_cs_t[h:h + 1, :]
                decay = jnp.exp(jnp.where(causal, diff, -jnp.inf))
                scores.append((cb * decay * dt_t[h:h + 1, :]).astype(bf16))
            lhs = jnp.concatenate(scores, axis=1)
            xp = xs_ref[rows, g * GROUP_COLS + pr * LANES:g * GROUP_COLS + (pr + 1) * LANES]
            rhs = jnp.concatenate([jnp.where(low_half, xp, 0.0), jnp.where(low_half, 0.0, xp)], axis=0)
            y_pairs.append(jnp.dot(lhs, rhs.astype(bf16), preferred_element_type=f32))
        y = jnp.concatenate(y_pairs, axis=1)
        xs_g = xs_ref[rows, gcols]
        xdw = (xs_g * wdt_ref[rows, gcols]).astype(bf16)
        st_new = lax.dot_general(bg, xdw, (((0,), (0,)), ((), ())), preferred_element_type=f32)
        prev = state_ref[g]
        y = y + jnp.dot(cg, prev.astype(bf16), preferred_element_type=f32) * eout_ref[rows, gcols]
        state_ref[g] = prev * eout_ref[last:last + 1, gcols] + st_new
        y = y + dskip_ref[:, gcols] * xs_g
        y = y * jax.nn.silu(z_ref[rows, gcols])
        y = y * lax.rsqrt(jnp.mean(y * y, axis=-1, keepdims=True) + RMS_EPS)
        o_ref[rows, gcols] = (y * ng_ref[:, gcols]).astype(o_ref.dtype)


def _ssd_kernel(xbc_ref, halo_ref, z_ref, dtr_ref, cw_ref, cb_ref, dtb_ref, alog_ref, dskip_ref,
                ng_ref, o_ref, state_ref, xs_ref, bm_ref, cm_ref, wdt_ref, eout_ref, expand_ref):
    step = pl.program_id(1)

    @pl.when(step == 0)
    def _():
        state_ref[...] = jnp.zeros_like(state_ref)
        hi = lax.broadcasted_iota(jnp.int32, (LANES, SSM_INNER), 0)
        ch = lax.broadcasted_iota(jnp.int32, (LANES, SSM_INNER), 1)
        expand_ref[...] = jnp.where(hi == ch // SSM_HEAD_DIM, 1.0, 0.0).astype(bf16)

    slab, rc = 256, 64
    n_state = SSM_GROUPS * SSM_STATE
    for s in range(SSM_CONV_DIM // slab):
        cols = slice(s * slab, (s + 1) * slab)
        w = cw_ref[:, cols]
        bias = cb_ref[:, cols]
        for r in range(xbc_ref.shape[0] // rc):
            rows = slice(r * rc, (r + 1) * rc)
            if r == 0:
                halo = jnp.where(step == 0, 0.0, halo_ref[:, cols])
            else:
                halo = xbc_ref[r * rc - CONV_HALO:r * rc, cols]
            cur = xbc_ref[rows, cols]
            e = jnp.concatenate([halo, cur], axis=0)
            acc = cur * w[SSM_CONV - 1:SSM_CONV, :]
            for k in range(1, SSM_CONV):
                acc = acc + pltpu.roll(e, k, axis=0)[CONV_HALO:, :] * w[SSM_CONV - 1 - k:SSM_CONV - k, :]
            act = jax.nn.silu(acc + bias)
            if s * slab < SSM_INNER:
                xs_ref[rows, cols] = act
            elif s * slab < SSM_INNER + n_state:
                off = s * slab - SSM_INNER
                bm_ref[rows, off:off + slab] = act.astype(bf16)
            else:
                off = s * slab - SSM_INNER - n_state
                cm_ref[rows, off:off + slab] = act.astype(bf16)

    ri = lax.broadcasted_iota(jnp.int32, (CHUNK, CHUNK), 0)
    ci = lax.broadcasted_iota(jnp.int32, (CHUNK, CHUNK), 1)
    causal = ri >= ci
    tril = jnp.where(causal, 1.0, 0.0).astype(bf16)
    low_half = lax.broadcasted_iota(jnp.int32, (CHUNK, LANES), 1) < SSM_HEAD_DIM
    consts = (causal, tril, expand_ref[...], low_half)
    for cc in range(xbc_ref.shape[0] // CHUNK):
        _ssd_chunk(slice(cc * CHUNK, (cc + 1) * CHUNK), dtr_ref, dtb_ref, alog_ref, z_ref, dskip_ref,
                   ng_ref, o_ref, state_ref, xs_ref, bm_ref, cm_ref, wdt_ref, eout_ref, consts)


def _ssd(proj, dtr, conv_w, conv_b, dt_bias, a_log, d_skip, norm_g, bsz, chunks_per_step=2):
    t = proj.shape[0]
    rb = chunks_per_step * CHUNK
    steps = SEQ // rb
    halo_blocks = rb // CONV_HALO
    row = lambda b, c: b * steps + c
    const = lambda b, c: (0, 0)
    return pl.pallas_call(
        _ssd_kernel,
        grid=(bsz, steps),
        in_specs=[
            pl.BlockSpec((rb, SSM_CONV_DIM), lambda b, c: (row(b, c), XBC_OFF // SSM_CONV_DIM)),
            pl.BlockSpec((CONV_HALO, SSM_CONV_DIM),
                         lambda b, c: (jnp.maximum(row(b, c) * halo_blocks - 1, 0), XBC_OFF // SSM_CONV_DIM)),
            pl.BlockSpec((rb, SSM_INNER), lambda b, c: (row(b, c), Z_OFF // SSM_INNER)),
            pl.BlockSpec((rb, DT_PAD), lambda b, c: (row(b, c), 0)),
            pl.BlockSpec((SSM_CONV, SSM_CONV_DIM), const),
            pl.BlockSpec((1, SSM_CONV_DIM), const),
            pl.BlockSpec((1, DT_PAD), const),
            pl.BlockSpec((1, DT_PAD), const),
            pl.BlockSpec((1, SSM_INNER), const),
            pl.BlockSpec((1, SSM_INNER), const),
        ],
        out_specs=pl.BlockSpec((rb, SSM_INNER), lambda b, c: (row(b, c), 0)),
        out_shape=jax.ShapeDtypeStruct((t, SSM_INNER), bf16),
        scratch_shapes=[
            pltpu.VMEM((SSM_GROUPS, SSM_STATE, GROUP_COLS), f32),
            pltpu.VMEM((rb, SSM_INNER), f32),
            pltpu.VMEM((rb, SSM_GROUPS * SSM_STATE), bf16),
            pltpu.VMEM((rb, SSM_GROUPS * SSM_STATE), bf16),
            pltpu.VMEM((rb, SSM_INNER), f32),
            pltpu.VMEM((rb, SSM_INNER), f32),
            pltpu.VMEM((LANES, SSM_INNER), bf16),
        ],
        compiler_params=_params("parallel", "arbitrary"),
        name="ssd",
    )(proj, proj, proj, dtr, conv_w, conv_b, dt_bias, a_log, d_skip, norm_g)


def _merge_kernel(a_ref, b_ref, c_ref, wa_ref, wb_ref, wc_ref, ga_ref, gb_ref, gc_ref, o_ref):
    ya = jnp.dot(a_ref[...], wa_ref[0], preferred_element_type=f32)
    yb = jnp.dot(b_ref[...], wb_ref[0], preferred_element_type=f32)
    yc = jnp.dot(c_ref[...], wc_ref[0], preferred_element_type=f32)
    merged = (jax.nn.sigmoid(ga_ref[...]) * ya + jax.nn.sigmoid(gb_ref[...]) * yb
              + jax.nn.sigmoid(gc_ref[...]) * yc)
    o_ref[...] = merged.astype(o_ref.dtype)


def _tile_grid(n_row, n_col, cols_outer):
    if cols_outer:
        return (n_col, n_row), lambda f: (lambda j, i: f(i, j))
    return (n_row, n_col), lambda f: f


def _merge(ma, gb_, yn, proj, wa, wb, wc, layer, tm=512, tn=1024, cols_outer=True):
    t = proj.shape[0]
    d = D_MODEL
    gate_blk = GATE_OFF // tn
    per_gate = d // tn
    grid, im = _tile_grid(t // tm, d // tn, cols_outer)
    return pl.pallas_call(
        _merge_kernel,
        grid=grid,
        in_specs=[
            pl.BlockSpec((tm, POOL_WIDTH), im(lambda i, j: (i, 0))),
            pl.BlockSpec((tm, GMLP_WIDTH), im(lambda i, j: (i, 0))),
            pl.BlockSpec((tm, SSM_INNER), im(lambda i, j: (i, 0))),
            pl.BlockSpec((1, POOL_WIDTH, tn), im(lambda i, j: (layer, 0, j))),
            pl.BlockSpec((1, GMLP_WIDTH, tn), im(lambda i, j: (layer, 0, j))),
            pl.BlockSpec((1, SSM_INNER, tn), im(lambda i, j: (layer, 0, j))),
            pl.BlockSpec((tm, tn), im(lambda i, j: (i, gate_blk + j))),
            pl.BlockSpec((tm, tn), im(lambda i, j: (i, gate_blk + per_gate + j))),
            pl.BlockSpec((tm, tn), im(lambda i, j: (i, gate_blk + 2 * per_gate + j))),
        ],
        out_specs=pl.BlockSpec((tm, tn), im(lambda i, j: (i, j))),
        out_shape=jax.ShapeDtypeStruct((t, d), bf16),
        compiler_params=_params("arbitrary", "arbitrary"),
        name="merge",
    )(ma, gb_, yn, wa, wb, wc, proj, proj, proj)


def _wo_kernel(m_ref, w_ref, x_ref, g_ref, o_ref):
    o_ref[...] = x_ref[...] + g_ref[0] * jnp.dot(m_ref[...], w_ref[0], preferred_element_type=f32)


def _wo(merged, w_o, xf, mod, layer, tm=512, tn=2048, cols_outer=True):
    t, d = xf.shape
    tiles_per_batch = SEQ // tm
    grid, im = _tile_grid(t // tm, d // tn, cols_outer)
    return pl.pallas_call(
        _wo_kernel,
        grid=grid,
        in_specs=[
            pl.BlockSpec((tm, d), im(lambda i, j: (i, 0))),
            pl.BlockSpec((1, d, tn), im(lambda i, j: (layer, 0, j))),
            pl.BlockSpec((tm, tn), im(lambda i, j: (i, j))),
            pl.BlockSpec((1, 1, tn), im(lambda i, j: ((i // tiles_per_batch) * N_ADA + G1, 0, j))),
        ],
        out_specs=pl.BlockSpec((tm, tn), im(lambda i, j: (i, j))),
        out_shape=jax.ShapeDtypeStruct((t, d), f32),
        compiler_params=_params("arbitrary", "arbitrary"),
        name="wo",
    )(merged, w_o, xf, mod)


def _mlp_kernel(x_ref, sh_ref, sc_ref, g_ref, wu_ref, wd_ref, fn_ref, o_ref, h_ref, *, final_norm):
    f = pl.program_id(1)

    @pl.when(f == 0)
    def _():
        _mod_rmsnorm(x_ref, sh_ref, sc_ref, h_ref)
        o_ref[...] = jnp.zeros_like(o_ref)

    u = jnp.dot(h_ref[...], wu_ref[0], preferred_element_type=f32)
    u = jnp.square(jnp.maximum(u, 0.0)).astype(bf16)
    o_ref[...] += jnp.dot(u, wd_ref[0], preferred_element_type=f32)

    @pl.when(f == pl.num_programs(1) - 1)
    def _():
        rows = 256
        gate = g_ref[0]

        def body(r, carry):
            rs = pl.ds(pl.multiple_of(r * rows, rows), rows)
            xn = x_ref[rs, :] + gate * o_ref[rs, :]
            if final_norm:
                ms = jnp.mean(xn * xn, axis=-1, keepdims=True)
                xn = (xn * lax.rsqrt(ms + RMS_EPS)) * fn_ref[...]
            o_ref[rs, :] = xn
            return carry

        lax.fori_loop(0, x_ref.shape[0] // rows, body, 0)


def _mlp(xf, mod, w_up, w_down, layer, fn, final_norm, tm=1024, tf=512):
    t, d = xf.shape
    tiles_per_batch = SEQ // tm
    mod_spec = lambda k: pl.BlockSpec((1, 1, d), lambda i, f: ((i // tiles_per_batch) * N_ADA + k, 0, 0))
    return pl.pallas_call(
        functools.partial(_mlp_kernel, final_norm=final_norm),
        grid=(t // tm, D_FF // tf),
        in_specs=[
            pl.BlockSpec((tm, d), lambda i, f: (i, 0)),
            mod_spec(SH2),
            mod_spec(SC2),
            mod_spec(G2),
            pl.BlockSpec((1, d, tf), lambda i, f: (layer, 0, f)),
            pl.BlockSpec((1, tf, d), lambda i, f: (layer, f, 0)),
            pl.BlockSpec((1, d), lambda i, f: (0, 0)),
        ],
        out_specs=pl.BlockSpec((tm, d), lambda i, f: (i, 0)),
        out_shape=jax.ShapeDtypeStruct((t, d), f32),
        scratch_shapes=[pltpu.VMEM((tm, d), bf16)],
        compiler_params=_params("parallel", "arbitrary"),
        name="mlp",
    )(xf, mod, mod, mod, w_up, w_down, fn)


WIN_TILE = 512
W_ALIGN = 32


def _win_src_row(j):
    c = j * WIN_TILE
    src = jnp.where(c < GATE_OFF, c - XBC_OFF + W_XBC,
                    jnp.where(c < UV_OFF, c - GATE_OFF + W_GATE,
                              jnp.where(c < Z_OFF, c - UV_OFF + W_UV,
                                        jnp.where(c < P_OFF, c - Z_OFF + W_Z, c - P_OFF))))
    return pl.multiple_of(src, W_ALIGN)


def _win_kernel(wt_ref, o_ref):
    o_ref[...] = wt_ref[0].T.astype(o_ref.dtype)


def _win_prep(w_in_t, layer):
    _, n, d = w_in_t.shape
    assert n == W_END and all(o % WIN_TILE == 0 for o in (XBC_OFF, GATE_OFF, UV_OFF, Z_OFF, P_OFF))
    return pl.pallas_call(
        _win_kernel,
        grid=(N_MAIN // WIN_TILE,),
        in_specs=[pl.BlockSpec((pl.Element(1), pl.Element(WIN_TILE), pl.Element(d)),
                               lambda j: (layer, _win_src_row(j), 0))],
        out_specs=pl.BlockSpec((d, WIN_TILE), lambda j: (0, j)),
        out_shape=jax.ShapeDtypeStruct((d, N_MAIN), bf16),
        compiler_params=_params("parallel"),
        name="winprep",
    )(w_in_t)


def _pad_lanes(v):
    return jnp.pad(v.astype(f32), (0, DT_PAD - v.shape[0])).reshape(1, DT_PAD)


def kernel(x, c, w_ada, b_ada, w_in, pool_w, pool_scale, gmlp_ln_g, gmlp_ln_b, gmlp_ws, gmlp_bs, conv_w, conv_b, dt_bias, a_log, d_skip, ssm_norm, w_pool_out, w_gmlp_out, w_ssm_out, w_o, w_up, w_down, final_norm):
    bsz, seq, d = x.shape
    assert (seq, d) == (SEQ, D_MODEL) and w_ada.shape[0] == DEPTH
    xf = x.reshape(bsz * seq, d)
    fn = final_norm.reshape(1, d)
    w_in_t = jnp.swapaxes(w_in, 1, 2)
    w_a, w_b, w_c = w_pool_out.astype(bf16), w_gmlp_out.astype(bf16), w_ssm_out.astype(bf16)
    w_out, w_up_b, w_down_b = w_o.astype(bf16), w_up.astype(bf16), w_down.astype(bf16)
    for l in range(DEPTH):
        mod = _ada(c, w_ada, b_ada, l, tk=256).reshape(bsz * N_ADA, 1, d)
        w_main = _win_prep(w_in_t, l)
        proj, dtr = _inproj(xf, mod, w_main, w_in_t, l)
        mixed_a = _pool(proj, pool_w[l].astype(bf16), pool_scale[l].reshape(1, POOL_WIDTH))
        gmlp_call = _gmlp_streamed if l == 0 else _gmlp
        gated_b = gmlp_call(proj, gmlp_ln_g[l].reshape(1, GMLP_WIDTH), gmlp_ln_b[l].reshape(1, GMLP_WIDTH),
                            gmlp_ws[l], gmlp_bs[l].reshape(GMLP_HEADS, CHUNK, 1))
        yn = _ssd(proj, dtr, conv_w[l], conv_b[l].reshape(1, SSM_CONV_DIM), _pad_lanes(dt_bias[l]),
                  _pad_lanes(a_log[l]), jnp.repeat(d_skip[l], SSM_HEAD_DIM).reshape(1, SSM_INNER),
                  ssm_norm[l].reshape(1, SSM_INNER), bsz)
        merged = _merge(mixed_a, gated_b, yn, proj, w_a, w_b, w_c, l)
        xf = _wo(merged, w_out, xf, mod, l)
        xf = _mlp(xf, mod, w_up_b, w_down_b, l, fn, final_norm=(l == DEPTH - 1))
    return xf.reshape(bsz, seq, d)
```
